```python
import jax
import jax.numpy as jnp
from jax import lax
import numpy as np

D_MODEL = 2048
BATCH = 4
SEQ = 8192
DEPTH = 2

HEAD_DIM = 128
RET_HEADS = 4
SB_HEADS = 4
SWA_Q_HEADS = 8
SWA_KV_HEADS = 2
D_MIX = (RET_HEADS + SB_HEADS + SWA_Q_HEADS) * HEAD_DIM
RET_CHUNK = 128
SB_BLOCK = 128
WINDOW = 128
CONV_WIDTH = 3
D_FF = 5632
RMS_EPS = 1e-6
GN_EPS = 1e-5
IN_SPLITS = (RET_HEADS * HEAD_DIM,) * 4 + (SB_HEADS * HEAD_DIM,) * 3 + (SWA_Q_HEADS * HEAD_DIM, SWA_KV_HEADS * HEAD_DIM, SWA_KV_HEADS * HEAD_DIM)
D_IN = sum(IN_SPLITS)
SPLIT_IDX = tuple(int(v) for v in np.cumsum(IN_SPLITS)[:-1])

kernel_name = "hymba_style_retention_stickbreak_swa_convffn"


def rms_norm(x, w):
    xf = x.astype(jnp.float32)
    y = xf * lax.rsqrt(jnp.mean(xf * xf, axis=-1, keepdims=True) + RMS_EPS)
    return (y * w.astype(jnp.float32)).astype(x.dtype)


def retention(q, k, v, g, gn_w):
    B, S, H, d = q.shape
    C = RET_CHUNK
    NC = S // C
    log_gamma = jnp.log(1.0 - 2.0 ** (-5.0 - jnp.arange(H, dtype=jnp.float32)))
    qc = q.astype(jnp.float32).reshape(B, NC, C, H, d)
    kc = (k.astype(jnp.float32) * d ** -0.5).reshape(B, NC, C, H, d)
    vc = v.astype(jnp.float32).reshape(B, NC, C, H, d)
    idx = jnp.arange(C, dtype=jnp.float32)
    diff = idx[:, None] - idx[None, :]
    decay_intra = jnp.where(diff >= 0, jnp.exp(log_gamma[:, None, None] * jnp.maximum(diff, 0.0)), 0.0)
    scores = jnp.einsum("bnihd,bnjhd->bnhij", qc, kc) * decay_intra
    o_intra = jnp.einsum("bnhij,bnjhe->bnihe", scores, vc)
    zeta = jnp.exp(log_gamma[:, None] * (C - 1.0 - idx)[None, :])
    chunk_state = jnp.einsum("bnjhd,hj,bnjhe->nbhde", kc, zeta, vc)
    chunk_decay = jnp.exp(log_gamma * C)[None, :, None, None]

    def step(state, contrib):
        return state * chunk_decay + contrib, state

    _, prev_state = lax.scan(step, jnp.zeros((B, H, d, d), jnp.float32), chunk_state)
    xi = jnp.exp(log_gamma[:, None] * (idx + 1.0)[None, :])
    o_cross = jnp.einsum("bnihd,hi,nbhde->bnihe", qc, xi, prev_state)
    o = (o_intra + o_cross).reshape(B, S, H, d)
    mu = jnp.mean(o, axis=-1, keepdims=True)
    var = jnp.mean(jnp.square(o - mu), axis=-1, keepdims=True)
    o = ((o - mu) * lax.rsqrt(var + GN_EPS)).reshape(B, S, H * d) * gn_w.astype(jnp.float32)
    return (jax.nn.silu(g.astype(jnp.float32)) * o).astype(q.dtype)


def stick_breaking(q, k, v):
    B, S, H, d = q.shape
    Q = SB_BLOCK
    NB = S // Q
    scale = d ** -0.5
    q_blocks = jnp.moveaxis(q.reshape(B, NB, Q, H, d), 1, 0)
    vf = v.astype(jnp.float32)
    key_pos = jnp.arange(S)

    def block(args):
        q_blk, blk = args
        z = jnp.einsum("bqhd,bkhd->bhqk", q_blk, k).astype(jnp.float32) * scale
        query_pos = blk * Q + jnp.arange(Q)
        causal = key_pos[None, :] < query_pos[:, None]
        log_beta = jax.nn.log_sigmoid(z)
        log_keep = jnp.where(causal, log_beta - z, 0.0)
        log_survive = lax.cumsum(log_keep, axis=3, reverse=True) - log_keep
        a = jnp.where(causal, jnp.exp(log_beta + log_survive), 0.0)
        return jnp.einsum("bhqk,bkhd->bqhd", a, vf)

    o = lax.map(block, (q_blocks, jnp.arange(NB)))
    return jnp.moveaxis(o, 0, 1).reshape(B, S, H * d).astype(q.dtype)


def sliding_window_sinks(q, k, v, sinks):
    B, S, Hq, d = q.shape
    Hkv = k.shape[2]
    G = Hq // Hkv
    W = WINDOW
    NB = S // W
    qb = q.reshape(B, NB, W, Hkv, G, d)

    def band(t):
        tb = t.reshape(B, NB, W, Hkv, d)
        prev = jnp.pad(tb[:, :-1], ((0, 0), (1, 0), (0, 0), (0, 0), (0, 0)))
        return jnp.concatenate([prev, tb], axis=2)

    kb, vb = band(k), band(v)
    s = jnp.einsum("bnqhgd,bnkhd->bnhgqk", qb, kb).astype(jnp.float32) * d ** -0.5
    qi = jnp.arange(W)[:, None]
    kj = jnp.arange(2 * W)[None, :]
    dist = qi + W - kj
    blk = jnp.arange(NB)[:, None, None]
    valid = (dist >= 0) & (dist < W) & (blk * W + kj - W >= 0)
    slopes = (2.0 ** (-(8.0 / Hq) * (jnp.arange(Hq, dtype=jnp.float32) + 1.0))).reshape(Hkv, G)
    s = s - slopes[:, :, None, None] * dist.astype(jnp.float32)
    s = jnp.where(valid[None, :, None, None], s, -jnp.inf)
    sink = sinks.astype(jnp.float32).reshape(Hkv, G)[:, :, None, None]
    m = jnp.maximum(jnp.max(s, axis=-1, keepdims=True), sink)
    p = jnp.exp(s - m)
    denom = jnp.sum(p, axis=-1, keepdims=True) + jnp.exp(sink - m)
    o = jnp.einsum("bnhgqk,bnkhd->bnqhgd", p / denom, vb.astype(jnp.float32))
    return o.reshape(B, S, Hq * d).astype(q.dtype)


def hybrid_mixer(h, w_in, w_out, ret_gn_w, swa_sinks):
    B, S, _ = h.shape
    proj = h @ w_in
    rq, rk, rv, rg, sq, sk, sv, aq, ak, av = jnp.split(proj, SPLIT_IDX, axis=-1)

    def heads(t, n):
        return t.reshape(B, S, n, HEAD_DIM)

    o_ret = retention(heads(rq, RET_HEADS), heads(rk, RET_HEADS), heads(rv, RET_HEADS), rg, ret_gn_w)
    o_sb = stick_breaking(heads(sq, SB_HEADS), heads(sk, SB_HEADS), heads(sv, SB_HEADS))
    o_swa = sliding_window_sinks(heads(aq, SWA_Q_HEADS), heads(ak, SWA_KV_HEADS), heads(av, SWA_KV_HEADS), swa_sinks)
    return jnp.concatenate([o_ret, o_sb, o_swa], axis=-1) @ w_out


def conv_ffn(h, w_up, conv_w, conv_b, w_down):
    S = h.shape[1]
    a, b = jnp.split(h @ w_up, 2, axis=-1)
    a_pad = jnp.pad(a, ((0, 0), (CONV_WIDTH - 1, 0), (0, 0)))
    a_conv = conv_b
    for i in range(CONV_WIDTH):
        a_conv = a_conv + a_pad[:, i:i + S] * conv_w[i]
    return (jax.nn.gelu(a_conv, approximate=True) * b) @ w_down


def setup_inputs(seed: int = 0) -> dict:
    key = jax.random.key(seed)
    ks = jax.random.split(key, 13)
    L = DEPTH

    def normal(k, shape, scale):
        return jax.random.normal(k, shape, jnp.float32) * scale

    def gain(k, shape):
        return 1.0 + 0.02 * jax.random.normal(k, shape, jnp.float32)

    return {
        "x": normal(ks[0], (BATCH, SEQ, D_MODEL), 1.0),
        "w_in": normal(ks[1], (L, D_MODEL, D_IN), D_MODEL ** -0.5),
        "w_out": normal(ks[2], (L, D_MIX, D_MODEL), D_MIX ** -0.5),
        "ret_gn_w": gain(ks[3], (L, RET_HEADS * HEAD_DIM)),
        "swa_sinks": normal(ks[4], (L, SWA_Q_HEADS), 0.5),
        "norm_mix_pre": gain(ks[5], (L, D_MODEL)),
        "norm_mix_post": gain(ks[6], (L, D_MODEL)),
        "norm_ffn_pre": gain(ks[7], (L, D_MODEL)),
        "norm_ffn_post": gain(ks[8], (L, D_MODEL)),
        "w_up": normal(ks[9], (L, D_MODEL, 2 * D_FF), D_MODEL ** -0.5),
        "conv_w": normal(ks[10], (L, CONV_WIDTH, D_FF), CONV_WIDTH ** -0.5),
        "conv_b": normal(ks[11], (L, D_FF), 0.02),
        "w_down": normal(ks[12], (L, D_FF, D_MODEL), D_FF ** -0.5),
    }


def reference(x, w_in, w_out, ret_gn_w, swa_sinks, norm_mix_pre, norm_mix_post, norm_ffn_pre, norm_ffn_post, w_up, conv_w, conv_b, w_down):
    for l in range(DEPTH):
        h = rms_norm(x, norm_mix_pre[l])
        x = x + rms_norm(hybrid_mixer(h, w_in[l], w_out[l], ret_gn_w[l], swa_sinks[l]), norm_mix_post[l])
        h = rms_norm(x, norm_ffn_pre[l])
        x = x + rms_norm(conv_ffn(h, w_up[l], conv_w[l], conv_b[l], w_down[l]), norm_ffn_post[l])
    return x
```

```python
import functools
import math

import jax
import jax.numpy as jnp
import numpy as np
from jax import lax
from jax.experimental import pallas as pl
from jax.experimental.pallas import tpu as pltpu

HEAD_DIM = 128
RET_HEADS = 4
SB_HEADS = 4
SWA_Q_HEADS = 8
SWA_KV_HEADS = 2
SWA_GROUP = SWA_Q_HEADS // SWA_KV_HEADS
RET_CHUNK = 128
WINDOW = 128
CONV_WIDTH = 3
RMS_EPS = 1e-6
GN_EPS = 1e-5

COL_RET_Q, COL_RET_K, COL_RET_V, COL_RET_G = 0, 4, 8, 12
COL_SB_Q, COL_SB_K, COL_SB_V = 16, 20, 24
COL_SWA_Q, COL_SWA_K, COL_SWA_V = 28, 36, 38
D_RET = RET_HEADS * HEAD_DIM
D_SB = SB_HEADS * HEAD_DIM
D_SWA = SWA_Q_HEADS * HEAD_DIM

BF16_SUBLANES = 16
VMEM_LIMIT_BYTES = 56 * 1024 * 1024

SB_LOG_ZERO = -104.0

F32 = jnp.float32
BF16 = jnp.bfloat16


def _params(*semantics):
    return pltpu.CompilerParams(dimension_semantics=semantics, vmem_limit_bytes=VMEM_LIMIT_BYTES)


def _rms_scale(x, gain):
    ms = jnp.mean(x * x, axis=-1, keepdims=True)
    return x * lax.rsqrt(ms + RMS_EPS) * gain


def _dot(a, b):
    return jnp.dot(a, b, preferred_element_type=F32)


def _dot_nt(a, b):
    return lax.dot_general(a, b, (((1,), (1,)), ((), ())), preferred_element_type=F32)


def _dot_tn(a, b):
    return lax.dot_general(a, b, (((0,), (0,)), ((), ())), preferred_element_type=F32)


NORM_ROWS = 64


def _norm_rows_into(dst_ref, dst_row0, x_ref, gain_ref, rows):
    def body(r, carry):
        src = pl.ds(pl.multiple_of(r * NORM_ROWS, NORM_ROWS), NORM_ROWS)
        dst = pl.ds(pl.multiple_of(dst_row0 + r * NORM_ROWS, BF16_SUBLANES), NORM_ROWS)
        dst_ref[dst, :] = _rms_scale(x_ref[src, :], gain_ref[...]).astype(BF16)
        return carry
    lax.fori_loop(0, rows // NORM_ROWS, body, 0)


def _norm_matmul_kernel(x_ref, gain_ref, w_ref, o_ref, h_ref):
    @pl.when(pl.program_id(1) == 0)
    def _():
        _norm_rows_into(h_ref, 0, x_ref, gain_ref, x_ref.shape[0])
    o_ref[...] = _dot(h_ref[...], w_ref[...]).astype(o_ref.dtype)


def _norm_matmul(x, gain, w, *, tm, tn):
    m, d = x.shape
    n = w.shape[1]
    return pl.pallas_call(
        _norm_matmul_kernel,
        grid=(m // tm, n // tn),
        in_specs=[
            pl.BlockSpec((tm, d), lambda i, j: (i, 0)),
            pl.BlockSpec((1, d), lambda i, j: (0, 0)),
            pl.BlockSpec((d, tn), lambda i, j: (0, j)),
        ],
        out_specs=pl.BlockSpec((tm, tn), lambda i, j: (i, j)),
        out_shape=jax.ShapeDtypeStruct((m, n), BF16),
        scratch_shapes=[pltpu.VMEM((tm, d), BF16)],
        compiler_params=_params("parallel", "arbitrary"),
        name="norm_in_proj",
    )(x, gain.reshape(1, d), w)


def _retention_tables():
    c = RET_CHUNK
    h = np.arange(RET_HEADS, dtype=np.float64)
    log_gamma = np.log(1.0 - 2.0 ** (-5.0 - h))
    idx = np.arange(c, dtype=np.float64)
    diff = idx[:, None] - idx[None, :]
    scale = HEAD_DIM ** -0.5
    decay = np.where(diff >= 0, np.exp(log_gamma[:, None, None] * np.maximum(diff, 0.0)), 0.0) * scale
    xi = np.exp(log_gamma[:, None] * (idx + 1.0)[None, :])
    zeta = np.exp(log_gamma[:, None] * (c - 1.0 - idx)[None, :]) * scale
    chunk_decay = np.exp(log_gamma * c)
    ones = np.ones((RET_HEADS, c, HEAD_DIM))
    t = np.stack([decay, xi[:, :, None] * ones, zeta[:, :, None] * ones, chunk_decay[:, None, None] * ones], axis=1)
    return jnp.asarray(t, dtype=F32)


def _retention_kernel(q_ref, k_ref, v_ref, g_ref, tab_ref, gn_ref, o_ref, state_ref):
    @pl.when(pl.program_id(2) == 0)
    def _():
        state_ref[...] = jnp.zeros_like(state_ref)

    c = RET_CHUNK
    decay = tab_ref[0, 0]
    xi = tab_ref[0, 1]
    zeta = tab_ref[0, 2]
    chunk_decay = tab_ref[0, 3]
    gn_w = gn_ref[0]

    def chunk(n, carry):
        rows = pl.ds(pl.multiple_of(n * c, c), c)
        q = q_ref[rows, :]
        k = k_ref[rows, :]
        v = v_ref[rows, :]
        state = state_ref[...]
        scores = _dot_nt(q, k) * decay
        o = _dot(scores.astype(BF16), v) + _dot(q, state.astype(BF16)) * xi
        kz = (k.astype(F32) * zeta).astype(BF16)
        state_ref[...] = state * chunk_decay + _dot_tn(kz, v)
        mu = jnp.mean(o, axis=-1, keepdims=True)
        oc = o - mu
        var = jnp.mean(oc * oc, axis=-1, keepdims=True)
        on = oc * lax.rsqrt(var + GN_EPS) * gn_w
        g = g_ref[rows, :].astype(F32)
        o_ref[rows, :] = (g / (1.0 + jnp.exp(-g)) * on).astype(o_ref.dtype)
        return carry

    lax.fori_loop(0, q_ref.shape[0] // c, chunk, 0)


def _retention(proj, gn_w, batch, seq, *, rows):
    m = proj.shape[0]
    d = HEAD_DIM
    steps = seq // rows

    def col(base):
        return pl.BlockSpec((rows, d), lambda b, h, s: (b * steps + s, base + h))

    return pl.pallas_call(
        _retention_kernel,
        grid=(batch, RET_HEADS, steps),
        in_specs=[
            col(COL_RET_Q), col(COL_RET_K), col(COL_RET_V), col(COL_RET_G),
            pl.BlockSpec((1, 4, RET_CHUNK, d), lambda b, h, s: (h, 0, 0, 0)),
            pl.BlockSpec((1, 1, d), lambda b, h, s: (h, 0, 0)),
        ],
        out_specs=pl.BlockSpec((rows, d), lambda b, h, s: (b * steps + s, h)),
        out_shape=jax.ShapeDtypeStruct((m, D_RET), BF16),
        scratch_shapes=[pltpu.VMEM((d, d), F32)],
        compiler_params=_params("parallel", "parallel", "arbitrary"),
        name="retention",
    )(proj, proj, proj, proj, _retention_tables(), gn_w.reshape(RET_HEADS, 1, d))


SB_TILE = 256


def _sb_kernel(q_ref, k_ref, v_ref, tri_ref, o_ref, acc_ref, carry_ref):
    t = SB_TILE
    scale = HEAD_DIM ** -0.5
    step_row0 = pl.program_id(2) * q_ref.shape[0]

    def tile(q, key_block, diagonal):
        keys = pl.ds(pl.multiple_of(key_block * t, t), t)
        z = _dot_nt(q, k_ref[keys, :]) * scale
        e = jnp.log(1.0 + jnp.exp(-jnp.abs(z)))
        log_beta = jnp.minimum(z, 0.0) - e
        log_keep = log_beta - z
        if diagonal:
            causal = lax.broadcasted_iota(jnp.int32, (t, t), 1) < lax.broadcasted_iota(jnp.int32, (t, t), 0)
            log_keep = jnp.where(causal, log_keep, 0.0)
        hi = log_keep.astype(BF16)
        lo = (log_keep - hi.astype(F32)).astype(BF16)
        log_survive = _dot(hi, tri_ref[...]) + _dot(lo, tri_ref[...]) + carry_ref[...]
        a = jnp.exp(log_beta + log_survive)
        if diagonal:
            a = jnp.where(causal, a, 0.0)
        acc_ref[...] += _dot(a.astype(BF16), v_ref[keys, :])
        carry_ref[...] += jnp.sum(log_keep, axis=-1, keepdims=True)

    def sub_tile(i, carry):
        rows = pl.ds(pl.multiple_of(i * t, t), t)
        q = q_ref[rows, :]
        diag_block = (step_row0 + i * t) // t
        acc_ref[...] = jnp.zeros_like(acc_ref)
        carry_ref[...] = jnp.zeros_like(carry_ref)
        tile(q, diag_block, True)

        def cond(state):
            key_block, max_carry = state
            return jnp.logical_and(key_block >= 0, max_carry > SB_LOG_ZERO)

        def body(state):
            key_block, _ = state
            tile(q, key_block, False)
            return key_block - 1, jnp.max(carry_ref[...])

        lax.while_loop(cond, body, (diag_block - 1, jnp.max(carry_ref[...])))
        o_ref[rows, :] = acc_ref[...].astype(o_ref.dtype)
        return carry

    lax.fori_loop(0, q_ref.shape[0] // t, sub_tile, 0)


def _stick_breaking(proj, batch, seq, *, rows):
    m = proj.shape[0]
    d = HEAD_DIM
    t = SB_TILE
    steps = seq // rows
    idx = np.arange(t)
    tri = jnp.asarray(idx[:, None] > idx[None, :], dtype=BF16)
    return pl.pallas_call(
        _sb_kernel,
        grid=(batch, SB_HEADS, steps),
        in_specs=[
            pl.BlockSpec((rows, d), lambda b, h, s: (b * steps + s, COL_SB_Q + h)),
            pl.BlockSpec((seq, d), lambda b, h, s: (b, COL_SB_K + h)),
            pl.BlockSpec((seq, d), lambda b, h, s: (b, COL_SB_V + h)),
            pl.BlockSpec((t, t), lambda b, h, s: (0, 0)),
        ],
        out_specs=pl.BlockSpec((rows, d), lambda b, h, s: (b * steps + s, h)),
        out_shape=jax.ShapeDtypeStruct((m, D_SB), BF16),
        scratch_shapes=[pltpu.VMEM((t, d), F32), pltpu.VMEM((t, 1), F32)],
        compiler_params=_params("parallel", "parallel", "arbitrary"),
        name="stick_breaking",
    )(proj, proj, proj, tri)


def _swa_bias():
    w = WINDOW
    qi = np.arange(w)[:, None]
    kj = np.arange(2 * w)[None, :]
    dist = qi + w - kj
    valid = (dist >= 0) & (dist < w)
    slopes = 2.0 ** (-(8.0 / SWA_Q_HEADS) * (np.arange(SWA_Q_HEADS, dtype=np.float64) + 1.0))
    bias = np.where(valid[None], -slopes[:, None, None] * dist[None].astype(np.float64), -np.inf)
    return jnp.asarray(bias, dtype=F32)


def _swa_kernel(sink_ref, q_ref, kc_ref, vc_ref, kp_ref, vp_ref, bias_ref, o_ref):
    w = WINDOW
    d = HEAD_DIM
    scale = d ** -0.5
    kv_head = pl.program_id(1)
    has_prev = pl.program_id(2) > 0
    for blk in range(q_ref.shape[0] // w):
        rows = pl.ds(blk * w, w)
        k_cur = kc_ref[rows, :]
        v_cur = vc_ref[rows, :]
        if blk == 0:
            k_prev, v_prev = kp_ref[...], vp_ref[...]
        else:
            k_prev, v_prev = kc_ref[pl.ds((blk - 1) * w, w), :], vc_ref[pl.ds((blk - 1) * w, w), :]
        for g in range(SWA_GROUP):
            q = q_ref[rows, pl.ds(g * d, d)]
            s_prev = _dot_nt(q, k_prev) * scale + bias_ref[g, :, pl.ds(0, w)]
            s_cur = _dot_nt(q, k_cur) * scale + bias_ref[g, :, pl.ds(w, w)]
            if blk == 0:
                s_prev = jnp.where(has_prev, s_prev, -jnp.inf)
            sink = sink_ref[kv_head * SWA_GROUP + g]
            mx = jnp.maximum(jnp.max(s_prev, axis=-1, keepdims=True), jnp.max(s_cur, axis=-1, keepdims=True))
            mx = jnp.maximum(mx, sink)
            p_prev = jnp.exp(s_prev - mx)
            p_cur = jnp.exp(s_cur - mx)
            denom = (jnp.sum(p_prev, axis=-1, keepdims=True) + jnp.sum(p_cur, axis=-1, keepdims=True)
                     + jnp.exp(sink - mx))
            o = _dot(p_prev.astype(BF16), v_prev) + _dot(p_cur.astype(BF16), v_cur)
            o_ref[rows, pl.ds(g * d, d)] = (o / denom).astype(o_ref.dtype)


def _sliding_window(proj, sinks, batch, seq, *, rows):
    m = proj.shape[0]
    d = HEAD_DIM
    w = WINDOW
    steps = seq // rows
    blocks_per_step = rows // w
    blocks_per_seq = seq // w

    def cur(base):
        return pl.BlockSpec((rows, d), lambda b, h, s, sink: (b * steps + s, base + h))

    def prev(base):
        return pl.BlockSpec(
            (w, d), lambda b, h, s, sink: (b * blocks_per_seq + jnp.maximum(s * blocks_per_step - 1, 0), base + h))

    grid_spec = pltpu.PrefetchScalarGridSpec(
        num_scalar_prefetch=1,
        grid=(batch, SWA_KV_HEADS, steps),
        in_specs=[
            pl.BlockSpec((rows, SWA_GROUP * d), lambda b, h, s, sink: (b * steps + s, COL_SWA_Q // SWA_GROUP + h)),
            cur(COL_SWA_K), cur(COL_SWA_V), prev(COL_SWA_K), prev(COL_SWA_V),
            pl.BlockSpec((SWA_GROUP, w, 2 * w), lambda b, h, s, sink: (h, 0, 0)),
        ],
        out_specs=pl.BlockSpec((rows, SWA_GROUP * d), lambda b, h, s, sink: (b * steps + s, h)),
    )
    return pl.pallas_call(
        _swa_kernel,
        grid_spec=grid_spec,
        out_shape=jax.ShapeDtypeStruct((m, D_SWA), BF16),
        compiler_params=_params("parallel", "parallel", "arbitrary"),
        name="sliding_window",
    )(sinks.astype(F32), proj, proj, proj, proj, proj, _swa_bias())


def _out_proj_kernel(ret_ref, sb_ref, swa_ref, w_ref, gain_ref, x_ref, o_ref):
    y = _dot(ret_ref[...], w_ref[pl.ds(0, D_RET), :])
    y += _dot(sb_ref[...], w_ref[pl.ds(D_RET, D_SB), :])
    y += _dot(swa_ref[...], w_ref[pl.ds(D_RET + D_SB, D_SWA), :])
    o_ref[...] = x_ref[...] + _rms_scale(y, gain_ref[...])


def _out_proj(o_ret, o_sb, o_swa, w, gain, x, *, tm):
    m, d = x.shape

    def rows(width):
        return pl.BlockSpec((tm, width), lambda i: (i, 0))

    return pl.pallas_call(
        _out_proj_kernel,
        grid=(m // tm,),
        in_specs=[
            rows(D_RET), rows(D_SB), rows(D_SWA),
            pl.BlockSpec(w.shape, lambda i: (0, 0)),
            pl.BlockSpec((1, d), lambda i: (0, 0)),
            rows(d),
        ],
        out_specs=rows(d),
        out_shape=jax.ShapeDtypeStruct((m, d), F32),
        compiler_params=_params("parallel"),
        name="out_proj_norm_res",
    )(o_ret, o_sb, o_swa, w, gain.reshape(1, d), x)


HALO = BF16_SUBLANES


def _gelu_tanh(x):
    return 0.5 * x * (1.0 + jnp.tanh(math.sqrt(2.0 / math.pi) * (x + 0.044715 * (x * x * x))))


def _ffn_up_kernel(x_ref, halo_ref, gain_ref, wa_ref, wb_ref, cw_ref, cb_ref, o_ref, h_ref, a_ref, *, tiles_per_seq):
    tm = x_ref.shape[0]

    @pl.when(pl.program_id(1) == 0)
    def _():
        _norm_rows_into(h_ref, HALO, x_ref, gain_ref, tm)
        keep = (pl.program_id(0) % tiles_per_seq != 0).astype(F32)
        h_ref[pl.ds(0, HALO), :] = (_rms_scale(halo_ref[...], gain_ref[...]) * keep).astype(BF16)

    a_ref[...] = _dot(h_ref[...], wa_ref[...])
    b = _dot(h_ref[pl.ds(HALO, tm), :], wb_ref[...])
    a_conv = cb_ref[...]
    for tap in range(CONV_WIDTH):
        a_conv = a_conv + a_ref[pl.ds(HALO - (CONV_WIDTH - 1) + tap, tm), :] * cw_ref[pl.ds(tap, 1), :]
    o_ref[...] = (_gelu_tanh(a_conv) * b).astype(o_ref.dtype)


def _ffn_up(x, gain, w_up, conv_w, conv_b, seq, *, tm, tn):
    m, d = x.shape
    f = conv_w.shape[1]
    nf = f // tn
    halo_blocks_per_tile = tm // HALO
    return pl.pallas_call(
        functools.partial(_ffn_up_kernel, tiles_per_seq=seq // tm),
        grid=(m // tm, nf),
        in_specs=[
            pl.BlockSpec((tm, d), lambda i, j: (i, 0)),
            pl.BlockSpec((HALO, d), lambda i, j: (jnp.maximum(i * halo_blocks_per_tile - 1, 0), 0)),
            pl.BlockSpec((1, d), lambda i, j: (0, 0)),
            pl.BlockSpec((d, tn), lambda i, j: (0, j)),
            pl.BlockSpec((d, tn), lambda i, j: (0, nf + j)),
            pl.BlockSpec((CONV_WIDTH, tn), lambda i, j: (0, j)),
            pl.BlockSpec((1, tn), lambda i, j: (0, j)),
        ],
        out_specs=pl.BlockSpec((tm, tn), lambda i, j: (i, j)),
        out_shape=jax.ShapeDtypeStruct((m, f), BF16),
        scratch_shapes=[pltpu.VMEM((tm + HALO, d), BF16), pltpu.VMEM((tm + HALO, tn), F32)],
        compiler_params=_params("parallel", "arbitrary"),
        name="ffn_up_conv_gate",
    )(x, x, gain.reshape(1, d), w_up, w_up, conv_w, conv_b.reshape(1, f))


def _ffn_down_kernel(g_ref, w_ref, gain_ref, x_ref, o_ref, acc_ref):
    k = pl.program_id(1)

    @pl.when(k == 0)
    def _():
        acc_ref[...] = jnp.zeros_like(acc_ref)

    acc_ref[...] += _dot(g_ref[...], w_ref[...])

    @pl.when(k == pl.num_programs(1) - 1)
    def _():
        o_ref[...] = x_ref[...] + _rms_scale(acc_ref[...], gain_ref[...])


def _ffn_down(g, w, gain, x, *, tm, tk):
    m, d = x.shape
    f = g.shape[1]
    return pl.pallas_call(
        _ffn_down_kernel,
        grid=(m // tm, f // tk),
        in_specs=[
            pl.BlockSpec((tm, tk), lambda i, k: (i, k)),
            pl.BlockSpec((tk, d), lambda i, k: (k, 0)),
            pl.BlockSpec((1, d), lambda i, k: (0, 0)),
            pl.BlockSpec((tm, d), lambda i, k: (i, 0)),
        ],
        out_specs=pl.BlockSpec((tm, d), lambda i, k: (i, 0)),
        out_shape=jax.ShapeDtypeStruct((m, d), F32),
        scratch_shapes=[pltpu.VMEM((tm, d), F32)],
        compiler_params=_params("parallel", "arbitrary"),
        name="ffn_down_norm_res",
    )(g, w, gain.reshape(1, d), x)


def _tile(extent, preferred):
    t = min(extent, preferred)
    while extent % t:
        t //= 2
    return t


def kernel(x, w_in, w_out, ret_gn_w, swa_sinks, norm_mix_pre, norm_mix_post, norm_ffn_pre, norm_ffn_post, w_up, conv_w, conv_b, w_down):
    batch, seq, d_model = x.shape
    depth = w_in.shape[0]
    d_ff = conv_w.shape[-1]
    assert seq % SB_TILE == 0 and d_ff % 512 == 0
    xf = x.reshape(batch * seq, d_model).astype(F32)
    tm_big = _tile(seq, 1024)
    tm_small = _tile(seq, 512)
    for l in range(depth):
        proj = _norm_matmul(xf, norm_mix_pre[l], w_in[l].astype(BF16), tm=tm_big, tn=1024)
        o_ret = _retention(proj, ret_gn_w[l], batch, seq, rows=tm_big)
        o_sb = _stick_breaking(proj, batch, seq, rows=tm_big)
        o_swa = _sliding_window(proj, swa_sinks[l], batch, seq, rows=tm_small)
        xf = _out_proj(o_ret, o_sb, o_swa, w_out[l].astype(BF16), norm_mix_post[l], xf, tm=tm_small)
        g = _ffn_up(xf, norm_ffn_pre[l], w_up[l].astype(BF16), conv_w[l], conv_b[l], seq, tm=tm_big, tn=512)
        xf = _ffn_down(g, w_down[l].astype(BF16), norm_ffn_post[l], xf, tm=tm_small, tk=512)
    return xf.reshape(batch, seq, d_model).astype(x.dtype)
```

```python
import functools
import math

import jax
import jax.numpy as jnp
import numpy as np
from jax import lax
from jax.experimental import pallas as pl
from jax.experimental.pallas import tpu as pltpu

HEAD_DIM = 128
RET_HEADS = 4
SB_HEADS = 4
SWA_Q_HEADS = 8
SWA_KV_HEADS = 2
SWA_GROUP = SWA_Q_HEADS // SWA_KV_HEADS
RET_CHUNK = 128
WINDOW = 128
CONV_WIDTH = 3
RMS_EPS = 1e-6
GN_EPS = 1e-5

COL_RET_Q, COL_RET_K, COL_RET_V, COL_RET_G = 0, 4, 8, 12
COL_SB_Q, COL_SB_K, COL_SB_V = 16, 20, 24
COL_SWA_Q, COL_SWA_K, COL_SWA_V = 28, 36, 38
D_RET = RET_HEADS * HEAD_DIM
D_SB = SB_HEADS * HEAD_DIM
D_SWA = SWA_Q_HEADS * HEAD_DIM

BF16_SUBLANES = 16
VMEM_LIMIT_BYTES = 56 * 1024 * 1024

SB_LOG_ZERO = -104.0

F32 = jnp.float32
BF16 = jnp.bfloat16


def _params(*semantics):
    return pltpu.CompilerParams(dimension_semantics=semantics, vmem_limit_bytes=VMEM_LIMIT_BYTES)


def _rms_scale(x, gain):
    ms = jnp.mean(x * x, axis=-1, keepdims=True)
    return x * lax.rsqrt(ms + RMS_EPS) * gain


def _dot(a, b):
    return jnp.dot(a, b, preferred_element_type=F32)


def _dot_nt(a, b):
    return lax.dot_general(a, b, (((1,), (1,)), ((), ())), preferred_element_type=F32)


def _dot_tn(a, b):
    return lax.dot_general(a, b, (((0,), (0,)), ((), ())), preferred_element_type=F32)


NORM_ROWS = 64


def _norm_rows_into(dst_ref, dst_row0, x_ref, gain_ref, rows):
    def body(r, carry):
        src = pl.ds(pl.multiple_of(r * NORM_ROWS, NORM_ROWS), NORM_ROWS)
        dst = pl.ds(pl.multiple_of(dst_row0 + r * NORM_ROWS, BF16_SUBLANES), NORM_ROWS)
        dst_ref[dst, :] = _rms_scale(x_ref[src, :], gain_ref[...]).astype(BF16)
        return carry
    lax.fori_loop(0, rows // NORM_ROWS, body, 0)


def _norm_matmul_kernel(x_ref, gain_ref, w_ref, o_ref, h_ref):
    @pl.when(pl.program_id(1) == 0)
    def _():
        _norm_rows_into(h_ref, 0, x_ref, gain_ref, x_ref.shape[0])
    o_ref[...] = _dot(h_ref[...], w_ref[...]).astype(o_ref.dtype)


def _norm_matmul(x, gain, w, *, tm, tn):
    m, d = x.shape
    n = w.shape[1]
    return pl.pallas_call(
        _norm_matmul_kernel,
        grid=(m // tm, n // tn),
        in_specs=[
            pl.BlockSpec((tm, d), lambda i, j: (i, 0)),
            pl.BlockSpec((1, d), lambda i, j: (0, 0)),
            pl.BlockSpec((d, tn), lambda i, j: (0, j)),
        ],
        out_specs=pl.BlockSpec((tm, tn), lambda i, j: (i, j)),
        out_shape=jax.ShapeDtypeStruct((m, n), BF16),
        scratch_shapes=[pltpu.VMEM((tm, d), BF16)],
        compiler_params=_params("parallel", "arbitrary"),
        name="norm_in_proj",
    )(x, gain.reshape(1, d), w)


def _retention_tables():
    c = RET_CHUNK
    h = np.arange(RET_HEADS, dtype=np.float64)
    log_gamma = np.log(1.0 - 2.0 ** (-5.0 - h))
    idx = np.arange(c, dtype=np.float64)
    diff = idx[:, None] - idx[None, :]
    scale = HEAD_DIM ** -0.5
    decay = np.where(diff >= 0, np.exp(log_gamma[:, None, None] * np.maximum(diff, 0.0)), 0.0) * scale
    xi = np.exp(log_gamma[:, None] * (idx + 1.0)[None, :])
    zeta = np.exp(log_gamma[:, None] * (c - 1.0 - idx)[None, :]) * scale
    chunk_decay = np.exp(log_gamma * c)
    ones = np.ones((RET_HEADS, c, HEAD_DIM))
    t = np.stack([decay, xi[:, :, None] * ones, zeta[:, :, None] * ones, chunk_decay[:, None, None] * ones], axis=1)
    return jnp.asarray(t, dtype=F32)


def _retention_kernel(q_ref, k_ref, v_ref, g_ref, tab_ref, gn_ref, o_ref, state_ref):
    @pl.when(pl.program_id(2) == 0)
    def _():
        state_ref[...] = jnp.zeros_like(state_ref)

    c = RET_CHUNK
    decay = tab_ref[0, 0]
    xi = tab_ref[0, 1]
    zeta = tab_ref[0, 2]
    chunk_decay = tab_ref[0, 3]
    gn_w = gn_ref[0]

    state = state_ref[...]
    for n in range(q_ref.shape[0] // c):
        rows = pl.ds(n * c, c)
        q = q_ref[rows, :]
        k = k_ref[rows, :]
        v = v_ref[rows, :]
        scores = _dot_nt(q, k) * decay
        o = _dot(scores.astype(BF16), v) + _dot(q, state.astype(BF16)) * xi
        kz = (k.astype(F32) * zeta).astype(BF16)
        state = state * chunk_decay + _dot_tn(kz, v)
        mu = jnp.mean(o, axis=-1, keepdims=True)
        oc = o - mu
        var = jnp.mean(oc * oc, axis=-1, keepdims=True)
        on = oc * lax.rsqrt(var + GN_EPS) * gn_w
        g = g_ref[rows, :].astype(F32)
        o_ref[rows, :] = (g / (1.0 + jnp.exp(-g)) * on).astype(o_ref.dtype)
    state_ref[...] = state


def _retention(proj, gn_w, batch, seq, *, rows):
    m = proj.shape[0]
    d = HEAD_DIM
    steps = seq // rows

    def col(base):
        return pl.BlockSpec((rows, d), lambda b, h, s: (b * steps + s, base + h))

    return pl.pallas_call(
        _retention_kernel,
        grid=(batch, RET_HEADS, steps),
        in_specs=[
            col(COL_RET_Q), col(COL_RET_K), col(COL_RET_V), col(COL_RET_G),
            pl.BlockSpec((1, 4, RET_CHUNK, d), lambda b, h, s: (h, 0, 0, 0)),
            pl.BlockSpec((1, 1, d), lambda b, h, s: (h, 0, 0)),
        ],
        out_specs=pl.BlockSpec((rows, d), lambda b, h, s: (b * steps + s, h)),
        out_shape=jax.ShapeDtypeStruct((m, D_RET), BF16),
        scratch_shapes=[pltpu.VMEM((d, d), F32)],
        compiler_params=_params("parallel", "parallel", "arbitrary"),
        name="retention",
    )(proj, proj, proj, proj, _retention_tables(), gn_w.reshape(RET_HEADS, 1, d))


SB_TILE = 256


def _sb_kernel(q_ref, k_ref, v_ref, tri_ref, o_ref, acc_ref, carry_ref):
    t = SB_TILE
    scale = HEAD_DIM ** -0.5
    step_row0 = pl.program_id(2) * q_ref.shape[0]

    def log_gates(q, keys):
        z = _dot_nt(q, k_ref[keys, :]) * scale
        log_beta = jnp.minimum(z, 0.0) - jnp.log(1.0 + jnp.exp(-jnp.abs(z)))
        return log_beta, log_beta - z

    def later_keys_sum(log_keep):
        hi = log_keep.astype(BF16)
        lo = (log_keep - hi.astype(F32)).astype(BF16)
        return _dot(hi, tri_ref[...]) + _dot(lo, tri_ref[...])

    def sub_tile(i, carry):
        rows = pl.ds(pl.multiple_of(i * t, t), t)
        q = q_ref[rows, :]
        diag_block = (step_row0 + i * t) // t

        near = pl.ds(pl.multiple_of(diag_block * t, t), t)
        log_beta, log_keep = log_gates(q, near)
        causal = lax.broadcasted_iota(jnp.int32, (t, t), 1) < lax.broadcasted_iota(jnp.int32, (t, t), 0)
        log_keep = jnp.where(causal, log_keep, 0.0)
        a_near = jnp.where(causal, jnp.exp(log_beta + later_keys_sum(log_keep)), 0.0)
        carry_near = jnp.sum(log_keep, axis=-1, keepdims=True)

        has_far = (diag_block > 0).astype(F32)
        far = pl.ds(pl.multiple_of(jnp.maximum(diag_block - 1, 0) * t, t), t)
        log_beta, log_keep = log_gates(q, far)
        log_keep = log_keep * has_far
        a_far = jnp.exp(log_beta + later_keys_sum(log_keep) + carry_near) * has_far
        acc_ref[...] = _dot(a_near.astype(BF16), v_ref[near, :]) + _dot(a_far.astype(BF16), v_ref[far, :])
        carry_ref[...] = carry_near + jnp.sum(log_keep, axis=-1, keepdims=True)

        def cond(state):
            key_block, max_carry = state
            return jnp.logical_and(key_block >= 0, max_carry > SB_LOG_ZERO)

        def body(state):
            key_block, _ = state
            keys = pl.ds(pl.multiple_of(key_block * t, t), t)
            log_beta, log_keep = log_gates(q, keys)
            a = jnp.exp(log_beta + later_keys_sum(log_keep) + carry_ref[...])
            acc_ref[...] += _dot(a.astype(BF16), v_ref[keys, :])
            carry_ref[...] += jnp.sum(log_keep, axis=-1, keepdims=True)
            return key_block - 1, jnp.max(carry_ref[...])

        lax.while_loop(cond, body, (diag_block - 2, jnp.max(carry_ref[...])))
        o_ref[rows, :] = acc_ref[...].astype(o_ref.dtype)
        return carry

    lax.fori_loop(0, q_ref.shape[0] // t, sub_tile, 0)


def _stick_breaking(proj, batch, seq, *, rows):
    m = proj.shape[0]
    d = HEAD_DIM
    t = SB_TILE
    steps = seq // rows
    idx = np.arange(t)
    tri = jnp.asarray(idx[:, None] > idx[None, :], dtype=BF16)
    return pl.pallas_call(
        _sb_kernel,
        grid=(batch, SB_HEADS, steps),
        in_specs=[
            pl.BlockSpec((rows, d), lambda b, h, s: (b * steps + s, COL_SB_Q + h)),
            pl.BlockSpec((seq, d), lambda b, h, s: (b, COL_SB_K + h)),
            pl.BlockSpec((seq, d), lambda b, h, s: (b, COL_SB_V + h)),
            pl.BlockSpec((t, t), lambda b, h, s: (0, 0)),
        ],
        out_specs=pl.BlockSpec((rows, d), lambda b, h, s: (b * steps + s, h)),
        out_shape=jax.ShapeDtypeStruct((m, D_SB), BF16),
        scratch_shapes=[pltpu.VMEM((t, d), F32), pltpu.VMEM((t, 1), F32)],
        compiler_params=_params("parallel", "parallel", "arbitrary"),
        name="stick_breaking",
    )(proj, proj, proj, tri)


def _swa_bias():
    w = WINDOW
    qi = np.arange(w)[:, None]
    kj = np.arange(2 * w)[None, :]
    dist = qi + w - kj
    valid = (dist >= 0) & (dist < w)
    slopes = 2.0 ** (-(8.0 / SWA_Q_HEADS) * (np.arange(SWA_Q_HEADS, dtype=np.float64) + 1.0))
    bias = np.where(valid[None], -slopes[:, None, None] * dist[None].astype(np.float64), -np.inf)
    return jnp.asarray(bias, dtype=F32)


def _swa_kernel(sink_ref, q_ref, kc_ref, vc_ref, kp_ref, vp_ref, bias_ref, o_ref, kb_ref, vb_ref):
    w = WINDOW
    d = HEAD_DIM
    scale = d ** -0.5
    rows = q_ref.shape[0]
    kv_head = pl.program_id(1)
    has_prev = pl.program_id(2) > 0
    kb_ref[pl.ds(0, w), :] = kp_ref[...]
    kb_ref[pl.ds(w, rows), :] = kc_ref[...]
    vb_ref[pl.ds(0, w), pl.ds(0, d)] = vp_ref[...]
    vb_ref[pl.ds(w, rows), pl.ds(0, d)] = vc_ref[...]
    vb_ref[:, pl.ds(d, d)] = jnp.ones((rows + w, d), BF16)
    bias = bias_ref[0]
    sinks = [sink_ref[kv_head * SWA_GROUP + g] for g in range(SWA_GROUP)]
    for blk in range(rows // w):
        tok = pl.ds(blk * w, w)
        band = pl.ds(blk * w, 2 * w)
        q = jnp.concatenate([q_ref[tok, pl.ds(g * d, d)] for g in range(SWA_GROUP)], axis=0)
        s = _dot_nt(q, kb_ref[band, :]) * scale + bias
        if blk == 0:
            in_current = lax.broadcasted_iota(jnp.int32, s.shape, 1) >= w
            s = jnp.where(jnp.logical_or(in_current, has_prev), s, -jnp.inf)
        s_max = jnp.max(s, axis=-1, keepdims=True)
        heads = [slice(g * w, (g + 1) * w) for g in range(SWA_GROUP)]
        mx = [jnp.maximum(s_max[heads[g]], sinks[g]) for g in range(SWA_GROUP)]
        p = jnp.concatenate([jnp.exp(s[heads[g]] - mx[g]).astype(BF16) for g in range(SWA_GROUP)], axis=0)
        ov = _dot(p, vb_ref[band, :])
        for g in range(SWA_GROUP):
            denom = ov[heads[g], d:] + jnp.exp(sinks[g] - mx[g])
            o_ref[tok, pl.ds(g * d, d)] = (ov[heads[g], :d] / denom).astype(o_ref.dtype)


def _sliding_window(proj, sinks, batch, seq, *, rows):
    m = proj.shape[0]
    d = HEAD_DIM
    w = WINDOW
    steps = seq // rows
    blocks_per_step = rows // w
    blocks_per_seq = seq // w

    gw = SWA_GROUP * w

    def cur(base):
        return pl.BlockSpec((rows, d), lambda b, h, s, sink: (b * steps + s, base + h))

    def prev(base):
        return pl.BlockSpec(
            (w, d), lambda b, h, s, sink: (b * blocks_per_seq + jnp.maximum(s * blocks_per_step - 1, 0), base + h))

    grid_spec = pltpu.PrefetchScalarGridSpec(
        num_scalar_prefetch=1,
        grid=(batch, SWA_KV_HEADS, steps),
        in_specs=[
            pl.BlockSpec((rows, SWA_GROUP * d), lambda b, h, s, sink: (b * steps + s, COL_SWA_Q // SWA_GROUP + h)),
            cur(COL_SWA_K), cur(COL_SWA_V), prev(COL_SWA_K), prev(COL_SWA_V),
            pl.BlockSpec((1, gw, 2 * w), lambda b, h, s, sink: (h, 0, 0)),
        ],
        out_specs=pl.BlockSpec((rows, SWA_GROUP * d), lambda b, h, s, sink: (b * steps + s, h)),
        scratch_shapes=[pltpu.VMEM((rows + w, d), BF16), pltpu.VMEM((rows + w, 2 * d), BF16)],
    )
    return pl.pallas_call(
        _swa_kernel,
        grid_spec=grid_spec,
        out_shape=jax.ShapeDtypeStruct((m, D_SWA), BF16),
        compiler_params=_params("parallel", "parallel", "arbitrary"),
        name="sliding_window",
    )(sinks.astype(F32), proj, proj, proj, proj, proj, _swa_bias().reshape(SWA_KV_HEADS, gw, 2 * w))


def _out_proj_kernel(ret_ref, sb_ref, swa_ref, w_ref, gain_ref, x_ref, o_ref):
    y = _dot(ret_ref[...], w_ref[pl.ds(0, D_RET), :])
    y += _dot(sb_ref[...], w_ref[pl.ds(D_RET, D_SB), :])
    y += _dot(swa_ref[...], w_ref[pl.ds(D_RET + D_SB, D_SWA), :])
    o_ref[...] = x_ref[...] + _rms_scale(y, gain_ref[...])


def _out_proj(o_ret, o_sb, o_swa, w, gain, x, *, tm):
    m, d = x.shape

    def rows(width):
        return pl.BlockSpec((tm, width), lambda i: (i, 0))

    return pl.pallas_call(
        _out_proj_kernel,
        grid=(m // tm,),
        in_specs=[
            rows(D_RET), rows(D_SB), rows(D_SWA),
            pl.BlockSpec(w.shape, lambda i: (0, 0)),
            pl.BlockSpec((1, d), lambda i: (0, 0)),
            rows(d),
        ],
        out_specs=rows(d),
        out_shape=jax.ShapeDtypeStruct((m, d), F32),
        compiler_params=_params("parallel"),
        name="out_proj_norm_res",
    )(o_ret, o_sb, o_swa, w, gain.reshape(1, d), x)


HALO = BF16_SUBLANES


def _gelu_tanh(x):
    return 0.5 * x * (1.0 + jnp.tanh(math.sqrt(2.0 / math.pi) * (x + 0.044715 * (x * x * x))))


def _ffn_up_kernel(x_ref, halo_ref, gain_ref, wa_ref, wb_ref, cw_ref, cb_ref, o_ref, h_ref, a_ref, *, tiles_per_seq):
    tm = x_ref.shape[0]

    @pl.when(pl.program_id(1) == 0)
    def _():
        _norm_rows_into(h_ref, HALO, x_ref, gain_ref, tm)
        keep = (pl.program_id(0) % tiles_per_seq != 0).astype(F32)
        h_ref[pl.ds(0, HALO), :] = (_rms_scale(halo_ref[...], gain_ref[...]) * keep).astype(BF16)

    a_ref[...] = _dot(h_ref[...], wa_ref[...])
    b = _dot(h_ref[pl.ds(HALO, tm), :], wb_ref[...])
    a_conv = cb_ref[...]
    for tap in range(CONV_WIDTH):
        a_conv = a_conv + a_ref[pl.ds(HALO - (CONV_WIDTH - 1) + tap, tm), :] * cw_ref[pl.ds(tap, 1), :]
    o_ref[...] = (_gelu_tanh(a_conv) * b).astype(o_ref.dtype)


def _ffn_up(x, gain, w_up, conv_w, conv_b, seq, *, tm, tn):
    m, d = x.shape
    f = conv_w.shape[1]
    nf = f // tn
    halo_blocks_per_tile = tm // HALO
    return pl.pallas_call(
        functools.partial(_ffn_up_kernel, tiles_per_seq=seq // tm),
        grid=(m // tm, nf),
        in_specs=[
            pl.BlockSpec((tm, d), lambda i, j: (i, 0)),
            pl.BlockSpec((HALO, d), lambda i, j: (jnp.maximum(i * halo_blocks_per_tile - 1, 0), 0)),
            pl.BlockSpec((1, d), lambda i, j: (0, 0)),
            pl.BlockSpec((d, tn), lambda i, j: (0, j)),
            pl.BlockSpec((d, tn), lambda i, j: (0, nf + j)),
            pl.BlockSpec((CONV_WIDTH, tn), lambda i, j: (0, j)),
            pl.BlockSpec((1, tn), lambda i, j: (0, j)),
        ],
        out_specs=pl.BlockSpec((tm, tn), lambda i, j: (i, j)),
        out_shape=jax.ShapeDtypeStruct((m, f), BF16),
        scratch_shapes=[pltpu.VMEM((tm + HALO, d), BF16), pltpu.VMEM((tm + HALO, tn), F32)],
        compiler_params=_params("parallel", "arbitrary"),
        name="ffn_up_conv_gate",
    )(x, x, gain.reshape(1, d), w_up, w_up, conv_w, conv_b.reshape(1, f))


def _ffn_down_kernel(g_ref, w_ref, gain_ref, x_ref, o_ref, acc_ref):
    k = pl.program_id(1)

    @pl.when(k == 0)
    def _():
        acc_ref[...] = jnp.zeros_like(acc_ref)

    acc_ref[...] += _dot(g_ref[...], w_ref[...])

    @pl.when(k == pl.num_programs(1) - 1)
    def _():
        o_ref[...] = x_ref[...] + _rms_scale(acc_ref[...], gain_ref[...])


def _ffn_down(g, w, gain, x, *, tm, tk):
    m, d = x.shape
    f = g.shape[1]
    return pl.pallas_call(
        _ffn_down_kernel,
        grid=(m // tm, f // tk),
        in_specs=[
            pl.BlockSpec((tm, tk), lambda i, k: (i, k)),
            pl.BlockSpec((tk, d), lambda i, k: (k, 0)),
            pl.BlockSpec((1, d), lambda i, k: (0, 0)),
            pl.BlockSpec((tm, d), lambda i, k: (i, 0)),
        ],
        out_specs=pl.BlockSpec((tm, d), lambda i, k: (i, 0)),
        out_shape=jax.ShapeDtypeStruct((m, d), F32),
        scratch_shapes=[pltpu.VMEM((tm, d), F32)],
        compiler_params=_params("parallel", "arbitrary"),
        name="ffn_down_norm_res",
    )(g, w, gain.reshape(1, d), x)


def _tile(extent, preferred):
    t = min(extent, preferred)
    while extent % t:
        t //= 2
    return t


def kernel(x, w_in, w_out, ret_gn_w, swa_sinks, norm_mix_pre, norm_mix_post, norm_ffn_pre, norm_ffn_post, w_up, conv_w, conv_b, w_down):
    batch, seq, d_model = x.shape
    depth = w_in.shape[0]
    d_ff = conv_w.shape[-1]
    assert seq % SB_TILE == 0 and d_ff % 512 == 0
    xf = x.reshape(batch * seq, d_model).astype(F32)
    tm_big = _tile(seq, 1024)
    tm_small = _tile(seq, 512)
    for l in range(depth):
        proj = _norm_matmul(xf, norm_mix_pre[l], w_in[l].astype(BF16), tm=tm_small, tn=2560)
        o_ret = _retention(proj, ret_gn_w[l], batch, seq, rows=tm_big)
        o_sb = _stick_breaking(proj, batch, seq, rows=tm_big)
        o_swa = _sliding_window(proj, swa_sinks[l], batch, seq, rows=tm_small)
        xf = _out_proj(o_ret, o_sb, o_swa, w_out[l].astype(BF16), norm_mix_post[l], xf, tm=tm_small)
        g = _ffn_up(xf, norm_ffn_pre[l], w_up[l].astype(BF16), conv_w[l], conv_b[l], seq, tm=tm_big, tn=512)
        xf = _ffn_down(g, w_down[l].astype(BF16), norm_ffn_post[l], xf, tm=tm_small, tk=d_ff // 2)
    return xf.reshape(batch, seq, d_model).astype(x.dtype)
```

```python
import functools
import math

import jax
import jax.numpy as jnp
import numpy as np
from jax import lax
from jax.experimental import pallas as pl
from jax.experimental.pallas import tpu as pltpu

HEAD_DIM = 128
RET_HEADS = 4
SB_HEADS = 4
SWA_Q_HEADS = 8
SWA_KV_HEADS = 2
SWA_GROUP = SWA_Q_HEADS // SWA_KV_HEADS
RET_CHUNK = 128
WINDOW = 128
CONV_WIDTH = 3
RMS_EPS = 1e-6
GN_EPS = 1e-5

COL_RET_Q, COL_RET_K, COL_RET_V, COL_RET_G = 0, 4, 8, 12
COL_SB_Q, COL_SB_K, COL_SB_V = 16, 20, 24
COL_SWA_Q, COL_SWA_K, COL_SWA_V = 28, 36, 38
D_RET = RET_HEADS * HEAD_DIM
D_SB = SB_HEADS * HEAD_DIM
D_SWA = SWA_Q_HEADS * HEAD_DIM

BF16_SUBLANES = 16
VMEM_LIMIT_BYTES = 56 * 1024 * 1024

SB_LOG2_ZERO = -151.0

F32 = jnp.float32
BF16 = jnp.bfloat16


def _params(*semantics):
    return pltpu.CompilerParams(dimension_semantics=semantics, vmem_limit_bytes=VMEM_LIMIT_BYTES)


def _rms_scale(x, gain):
    ms = jnp.mean(x * x, axis=-1, keepdims=True)
    return x * lax.rsqrt(ms + RMS_EPS) * gain


def _dot(a, b):
    return jnp.dot(a, b, preferred_element_type=F32)


def _dot_nt(a, b):
    return lax.dot_general(a, b, (((1,), (1,)), ((), ())), preferred_element_type=F32)


def _dot_tn(a, b):
    return lax.dot_general(a, b, (((0,), (0,)), ((), ())), preferred_element_type=F32)


NORM_ROWS = 64


def _norm_rows_into(dst_ref, dst_row0, x_ref, gain_ref, rows):
    def body(r, carry):
        src = pl.ds(pl.multiple_of(r * NORM_ROWS, NORM_ROWS), NORM_ROWS)
        dst = pl.ds(pl.multiple_of(dst_row0 + r * NORM_ROWS, BF16_SUBLANES), NORM_ROWS)
        dst_ref[dst, :] = _rms_scale(x_ref[src, :], gain_ref[...]).astype(BF16)
        return carry
    lax.fori_loop(0, rows // NORM_ROWS, body, 0)


def _norm_matmul_kernel(x_ref, gain_ref, w_ref, o_ref, h_ref):
    @pl.when(pl.program_id(1) == 0)
    def _():
        _norm_rows_into(h_ref, 0, x_ref, gain_ref, x_ref.shape[0])
    o_ref[...] = _dot(h_ref[...], w_ref[...]).astype(o_ref.dtype)


def _matmul_kernel(h_ref, w_ref, o_ref):
    o_ref[...] = _dot(h_ref[...], w_ref[...]).astype(o_ref.dtype)


def _in_proj(x, gain, w, layer, *, tm, tn):
    m, d = x.shape
    n = w.shape[2]
    rows = pl.BlockSpec((tm, d), lambda i, j: (i, 0))
    w_spec = pl.BlockSpec((None, d, tn), lambda i, j: (layer, 0, j))
    common = dict(
        grid=(m // tm, n // tn),
        out_specs=pl.BlockSpec((tm, tn), lambda i, j: (i, j)),
        out_shape=jax.ShapeDtypeStruct((m, n), BF16),
        compiler_params=_params("parallel", "arbitrary"),
    )
    if gain is None:
        return pl.pallas_call(_matmul_kernel, in_specs=[rows, w_spec], name="in_proj", **common)(x, w)
    return pl.pallas_call(
        _norm_matmul_kernel,
        in_specs=[rows, pl.BlockSpec((1, d), lambda i, j: (0, 0)), w_spec],
        scratch_shapes=[pltpu.VMEM((tm, d), BF16)],
        name="norm_in_proj",
        **common,
    )(x, gain.reshape(1, d), w)


def _retention_tables():
    c = RET_CHUNK
    h = np.arange(RET_HEADS, dtype=np.float64)
    log_gamma = np.log(1.0 - 2.0 ** (-5.0 - h))
    idx = np.arange(c, dtype=np.float64)
    diff = idx[:, None] - idx[None, :]
    scale = HEAD_DIM ** -0.5
    decay = np.where(diff >= 0, np.exp(log_gamma[:, None, None] * np.maximum(diff, 0.0)), 0.0) * scale
    xi = np.exp(log_gamma[:, None] * (idx + 1.0)[None, :])
    zeta = np.exp(log_gamma[:, None] * (c - 1.0 - idx)[None, :]) * scale
    chunk_decay = np.exp(log_gamma * c)
    ones = np.ones((RET_HEADS, c, HEAD_DIM))
    t = np.stack([decay, xi[:, :, None] * ones, zeta[:, :, None] * ones, chunk_decay[:, None, None] * ones], axis=1)
    return jnp.asarray(t, dtype=F32)


def _retention_kernel(q_ref, k_ref, v_ref, g_ref, tab_ref, gn_ref, o_ref, state_ref):
    @pl.when(pl.program_id(2) == 0)
    def _():
        state_ref[...] = jnp.zeros_like(state_ref)

    c = RET_CHUNK
    decay = tab_ref[0, 0]
    xi = tab_ref[0, 1]
    zeta = tab_ref[0, 2]
    chunk_decay = tab_ref[0, 3]
    gn_w = gn_ref[0]

    chunks = [pl.ds(n * c, c) for n in range(q_ref.shape[0] // c)]
    scores = [(_dot_nt(q_ref[rows, :], k_ref[rows, :]) * decay).astype(BF16) for rows in chunks]
    contribs = [_dot_tn((k_ref[rows, :].astype(F32) * zeta).astype(BF16), v_ref[rows, :]) for rows in chunks]
    states = []
    state = state_ref[...]
    for contrib in contribs:
        states.append(state.astype(BF16))
        state = state * chunk_decay + contrib
    state_ref[...] = state
    outs = [_dot(s, v_ref[rows, :]) + _dot(q_ref[rows, :], st) * xi for rows, s, st in zip(chunks, scores, states)]
    for rows, o in zip(chunks, outs):
        mu = jnp.mean(o, axis=-1, keepdims=True)
        oc = o - mu
        var = jnp.mean(oc * oc, axis=-1, keepdims=True)
        on = oc * lax.rsqrt(var + GN_EPS) * gn_w
        g = g_ref[rows, :].astype(F32)
        o_ref[rows, :] = (g / (1.0 + jnp.exp(-g)) * on).astype(o_ref.dtype)


def _retention(proj, gn_w, batch, seq, *, rows):
    m = proj.shape[0]
    d = HEAD_DIM
    steps = seq // rows

    def col(base):
        return pl.BlockSpec((rows, d), lambda b, h, s: (b * steps + s, base + h))

    return pl.pallas_call(
        _retention_kernel,
        grid=(batch, RET_HEADS, steps),
        in_specs=[
            col(COL_RET_Q), col(COL_RET_K), col(COL_RET_V), col(COL_RET_G),
            pl.BlockSpec((1, 4, RET_CHUNK, d), lambda b, h, s: (h, 0, 0, 0)),
            pl.BlockSpec((1, 1, d), lambda b, h, s: (h, 0, 0)),
        ],
        out_specs=pl.BlockSpec((rows, d), lambda b, h, s: (b * steps + s, h)),
        out_shape=jax.ShapeDtypeStruct((m, D_RET), BF16),
        scratch_shapes=[pltpu.VMEM((d, d), F32)],
        compiler_params=_params("parallel", "parallel", "arbitrary"),
        name="retention",
    )(proj, proj, proj, proj, _retention_tables(), gn_w.reshape(RET_HEADS, 1, d))


SB_TILE = 256


def _sb_kernel(q_ref, k_ref, v_ref, tri_ref, o_ref, acc_ref, carry_ref):
    t = SB_TILE
    to_log2_logit = HEAD_DIM ** -0.5 * math.log2(math.e)
    first_block = pl.program_id(2) * (q_ref.shape[0] // t)

    def log_gates(q, keys):
        z = _dot_nt(q, k_ref[keys, :]) * to_log2_logit
        log_beta = jnp.minimum(z, 0.0) - jnp.log2(1.0 + jnp.exp2(-jnp.abs(z)))
        return log_beta, log_beta - z

    def later_keys_sum(log_keep):
        hi = log_keep.astype(BF16)
        lo = (log_keep - hi.astype(F32)).astype(BF16)
        return _dot(hi, tri_ref[...]) + _dot(lo, tri_ref[...])

    causal = lax.broadcasted_iota(jnp.int32, (t, t), 1) < lax.broadcasted_iota(jnp.int32, (t, t), 0)
    tiles = range(q_ref.shape[0] // t)
    rows = [pl.ds(i * t, t) for i in tiles]
    near = [pl.ds(pl.multiple_of((first_block + i) * t, t), t) for i in tiles]
    far = [pl.ds(pl.multiple_of(jnp.maximum(first_block + i - 1, 0) * t, t), t) for i in tiles]
    has_far = [(first_block + i > 0).astype(F32) for i in tiles]
    gates_near = [log_gates(q_ref[rows[i], :], near[i]) for i in tiles]
    gates_far = [log_gates(q_ref[rows[i], :], far[i]) for i in tiles]
    keep_near = [jnp.where(causal, gates_near[i][1], 0.0) for i in tiles]
    later_near = [later_keys_sum(keep_near[i]) for i in tiles]
    later_far = [later_keys_sum(gates_far[i][1]) for i in tiles]
    carry_near = [jnp.sum(keep_near[i], axis=-1, keepdims=True) for i in tiles]
    a_near = [jnp.where(causal, jnp.exp2(gates_near[i][0] + later_near[i]), 0.0).astype(BF16) for i in tiles]
    a_far = [jnp.exp2(gates_far[i][0] + later_far[i] + carry_near[i]).astype(BF16) for i in tiles]
    for i in tiles:
        v_far = v_ref[far[i], :] * has_far[i].astype(BF16)
        acc_ref[rows[i], :] = _dot(a_near[i], v_ref[near[i], :]) + _dot(a_far[i], v_far)
        carry_ref[rows[i], :] = carry_near[i] + jnp.sum(gates_far[i][1], axis=-1, keepdims=True) * has_far[i]

    @pl.when(jnp.max(carry_ref[...]) > SB_LOG2_ZERO)
    def _():
        def sub_tile(i, carry):
            rows = pl.ds(pl.multiple_of(i * t, t), t)
            q = q_ref[rows, :]

            def cond(state):
                key_block, max_carry = state
                return jnp.logical_and(key_block >= 0, max_carry > SB_LOG2_ZERO)

            def body(state):
                key_block, _ = state
                keys = pl.ds(pl.multiple_of(key_block * t, t), t)
                log_beta, log_keep = log_gates(q, keys)
                a = jnp.exp2(log_beta + later_keys_sum(log_keep) + carry_ref[rows, :])
                acc_ref[rows, :] += _dot(a.astype(BF16), v_ref[keys, :])
                carry_ref[rows, :] += jnp.sum(log_keep, axis=-1, keepdims=True)
                return key_block - 1, jnp.max(carry_ref[rows, :])

            lax.while_loop(cond, body, (first_block + i - 2, jnp.max(carry_ref[rows, :])))
            return carry

        lax.fori_loop(0, q_ref.shape[0] // t, sub_tile, 0)

    o_ref[...] = acc_ref[...].astype(o_ref.dtype)


def _stick_breaking(proj, batch, seq, *, rows):
    m = proj.shape[0]
    d = HEAD_DIM
    t = SB_TILE
    steps = seq // rows
    idx = np.arange(t)
    tri = jnp.asarray(idx[:, None] > idx[None, :], dtype=BF16)
    return pl.pallas_call(
        _sb_kernel,
        grid=(batch, SB_HEADS, steps),
        in_specs=[
            pl.BlockSpec((rows, d), lambda b, h, s: (b * steps + s, COL_SB_Q + h)),
            pl.BlockSpec((seq, d), lambda b, h, s: (b, COL_SB_K + h)),
            pl.BlockSpec((seq, d), lambda b, h, s: (b, COL_SB_V + h)),
            pl.BlockSpec((t, t), lambda b, h, s: (0, 0)),
        ],
        out_specs=pl.BlockSpec((rows, d), lambda b, h, s: (b * steps + s, h)),
        out_shape=jax.ShapeDtypeStruct((m, D_SB), BF16),
        scratch_shapes=[pltpu.VMEM((rows, d), F32), pltpu.VMEM((rows, 1), F32)],
        compiler_params=_params("parallel", "parallel", "arbitrary"),
        name="stick_breaking",
    )(proj, proj, proj, tri)


def _swa_bias():
    w = WINDOW
    qi = np.arange(w)[:, None]
    kj = np.arange(2 * w)[None, :]
    dist = qi + w - kj
    valid = (dist >= 0) & (dist < w)
    slopes = 2.0 ** (-(8.0 / SWA_Q_HEADS) * (np.arange(SWA_Q_HEADS, dtype=np.float64) + 1.0))
    bias = np.where(valid[None], -slopes[:, None, None] * dist[None].astype(np.float64), -np.inf)
    return jnp.asarray(bias, dtype=F32)


def _swa_kernel(sink_ref, q_ref, kc_ref, vc_ref, kp_ref, vp_ref, bias_ref, o_ref, kb_ref, vb_ref):
    w = WINDOW
    d = HEAD_DIM
    scale = d ** -0.5
    rows = q_ref.shape[0]
    kv_head = pl.program_id(1)
    has_prev = pl.program_id(2) > 0
    kb_ref[pl.ds(0, w), :] = kp_ref[...]
    kb_ref[pl.ds(w, rows), :] = kc_ref[...]
    vb_ref[pl.ds(0, w), pl.ds(0, d)] = vp_ref[...]
    vb_ref[pl.ds(w, rows), pl.ds(0, d)] = vc_ref[...]
    vb_ref[:, pl.ds(d, d)] = jnp.ones((rows + w, d), BF16)
    bias = bias_ref[0]
    sinks = [sink_ref[kv_head * SWA_GROUP + g] for g in range(SWA_GROUP)]
    blocks = range(rows // w)
    groups = range(SWA_GROUP)
    heads = [slice(g * w, (g + 1) * w) for g in groups]
    scores = []
    for blk in blocks:
        tok = pl.ds(blk * w, w)
        q = jnp.concatenate([q_ref[tok, pl.ds(g * d, d)] for g in groups], axis=0)
        s = _dot_nt(q, kb_ref[pl.ds(blk * w, 2 * w), :]) * scale + bias
        if blk == 0:
            in_current = lax.broadcasted_iota(jnp.int32, s.shape, 1) >= w
            s = jnp.where(jnp.logical_or(in_current, has_prev), s, -jnp.inf)
        scores.append(s)
    row_max = [jnp.max(s, axis=-1, keepdims=True) for s in scores]
    mx = [[jnp.maximum(row_max[blk][heads[g]], sinks[g]) for g in groups] for blk in blocks]
    probs = [jnp.concatenate([jnp.exp(scores[blk][heads[g]] - mx[blk][g]).astype(BF16) for g in groups], axis=0)
             for blk in blocks]
    outs = [_dot(probs[blk], vb_ref[pl.ds(blk * w, 2 * w), :]) for blk in blocks]
    for blk in blocks:
        for g in groups:
            denom = outs[blk][heads[g], d:] + jnp.exp(sinks[g] - mx[blk][g])
            o_ref[pl.ds(blk * w, w), pl.ds(g * d, d)] = (outs[blk][heads[g], :d] / denom).astype(o_ref.dtype)


def _sliding_window(proj, sinks, batch, seq, *, rows):
    m = proj.shape[0]
    d = HEAD_DIM
    w = WINDOW
    steps = seq // rows
    blocks_per_step = rows // w
    blocks_per_seq = seq // w

    gw = SWA_GROUP * w

    def cur(base):
        return pl.BlockSpec((rows, d), lambda b, h, s, sink: (b * steps + s, base + h))

    def prev(base):
        return pl.BlockSpec(
            (w, d), lambda b, h, s, sink: (b * blocks_per_seq + jnp.maximum(s * blocks_per_step - 1, 0), base + h))

    grid_spec = pltpu.PrefetchScalarGridSpec(
        num_scalar_prefetch=1,
        grid=(batch, SWA_KV_HEADS, steps),
        in_specs=[
            pl.BlockSpec((rows, SWA_GROUP * d), lambda b, h, s, sink: (b * steps + s, COL_SWA_Q // SWA_GROUP + h)),
            cur(COL_SWA_K), cur(COL_SWA_V), prev(COL_SWA_K), prev(COL_SWA_V),
            pl.BlockSpec((1, gw, 2 * w), lambda b, h, s, sink: (h, 0, 0)),
        ],
        out_specs=pl.BlockSpec((rows, SWA_GROUP * d), lambda b, h, s, sink: (b * steps + s, h)),
        scratch_shapes=[pltpu.VMEM((rows + w, d), BF16), pltpu.VMEM((rows + w, 2 * d), BF16)],
    )
    return pl.pallas_call(
        _swa_kernel,
        grid_spec=grid_spec,
        out_shape=jax.ShapeDtypeStruct((m, D_SWA), BF16),
        compiler_params=_params("parallel", "parallel", "arbitrary"),
        name="sliding_window",
    )(sinks.astype(F32), proj, proj, proj, proj, proj, _swa_bias().reshape(SWA_KV_HEADS, gw, 2 * w))


OUT_PROJ_ROWS = 128


def _out_proj_kernel(ret_ref, sb_ref, swa_ref, w_ref, gain_ref, next_gain_ref, x_ref, o_ref, h_ref):
    for r in range(x_ref.shape[0] // OUT_PROJ_ROWS):
        rows = pl.ds(r * OUT_PROJ_ROWS, OUT_PROJ_ROWS)
        y = _dot(ret_ref[rows, :], w_ref[pl.ds(0, D_RET), :])
        y += _dot(sb_ref[rows, :], w_ref[pl.ds(D_RET, D_SB), :])
        y += _dot(swa_ref[rows, :], w_ref[pl.ds(D_RET + D_SB, D_SWA), :])
        x = x_ref[rows, :] + _rms_scale(y, gain_ref[...])
        o_ref[rows, :] = x
        h_ref[rows, :] = _rms_scale(x, next_gain_ref[...]).astype(h_ref.dtype)


def _out_proj(o_ret, o_sb, o_swa, w, layer, gain, next_gain, x, *, tm):
    m, d = x.shape

    def rows(width):
        return pl.BlockSpec((tm, width), lambda i: (i, 0))

    vec = pl.BlockSpec((1, d), lambda i: (0, 0))
    return pl.pallas_call(
        _out_proj_kernel,
        grid=(m // tm,),
        in_specs=[
            rows(D_RET), rows(D_SB), rows(D_SWA),
            pl.BlockSpec((None,) + w.shape[1:], lambda i: (layer, 0, 0)),
            vec, vec, rows(d),
        ],
        out_specs=[rows(d), rows(d)],
        out_shape=[jax.ShapeDtypeStruct((m, d), F32), jax.ShapeDtypeStruct((m, d), BF16)],
        compiler_params=_params("parallel"),
        name="out_proj_norm_res",
    )(o_ret, o_sb, o_swa, w, gain.reshape(1, d), next_gain.reshape(1, d), x)


TAIL = 8
assert TAIL >= CONV_WIDTH - 1


def _gelu_tanh(x):
    return 0.5 * x * (1.0 + jnp.tanh(math.sqrt(2.0 / math.pi) * (x + 0.044715 * (x * x * x))))


def _ffn_up_kernel(h_ref, wa_ref, wb_ref, cw_ref, cb_ref, o_ref, a_ref, tail_ref, *, tiles_per_seq):
    tm = h_ref.shape[0]
    j = pl.program_id(1)
    starts_sequence = pl.program_id(0) % tiles_per_seq == 0

    @pl.when(starts_sequence)
    def _():
        a_ref[pl.ds(0, TAIL), :] = jnp.zeros((TAIL, a_ref.shape[1]), F32)

    @pl.when(jnp.logical_not(starts_sequence))
    def _():
        a_ref[pl.ds(0, TAIL), :] = tail_ref[j]

    a_ref[pl.ds(TAIL, tm), :] = _dot(h_ref[...], wa_ref[...])
    tail_ref[j] = a_ref[pl.ds(tm, TAIL), :]
    b = _dot(h_ref[...], wb_ref[...])
    a_conv = cb_ref[...]
    for tap in range(CONV_WIDTH):
        a_conv = a_conv + a_ref[pl.ds(TAIL - (CONV_WIDTH - 1) + tap, tm), :] * cw_ref[pl.ds(tap, 1), :]
    o_ref[...] = (_gelu_tanh(a_conv) * b).astype(o_ref.dtype)


def _ffn_up(h, w_up, layer, conv_w, conv_b, seq, *, tm, tn):
    m, d = h.shape
    f = conv_w.shape[1]
    nf = f // tn
    return pl.pallas_call(
        functools.partial(_ffn_up_kernel, tiles_per_seq=seq // tm),
        grid=(m // tm, nf),
        in_specs=[
            pl.BlockSpec((tm, d), lambda i, j: (i, 0)),
            pl.BlockSpec((None, d, tn), lambda i, j: (layer, 0, j)),
            pl.BlockSpec((None, d, tn), lambda i, j: (layer, 0, nf + j)),
            pl.BlockSpec((CONV_WIDTH, tn), lambda i, j: (0, j)),
            pl.BlockSpec((1, tn), lambda i, j: (0, j)),
        ],
        out_specs=pl.BlockSpec((tm, tn), lambda i, j: (i, j)),
        out_shape=jax.ShapeDtypeStruct((m, f), BF16),
        scratch_shapes=[pltpu.VMEM((TAIL + tm, tn), F32), pltpu.VMEM((nf, TAIL, tn), F32)],
        compiler_params=_params("arbitrary", "arbitrary"),
        name="ffn_up_conv_gate",
    )(h, w_up, w_up, conv_w, conv_b.reshape(1, f))


def _ffn_down_kernel(g_ref, w_ref, gain_ref, next_gain_ref, x_ref, o_ref, *rest):
    acc_ref = rest[-1]
    k = pl.program_id(1)
    last = pl.num_programs(1) - 1

    @pl.when(k == 0)
    def _():
        acc_ref[...] = _dot(g_ref[...], w_ref[...])

    @pl.when(jnp.logical_and(k > 0, k < last))
    def _():
        acc_ref[...] += _dot(g_ref[...], w_ref[...])

    @pl.when(k == last)
    def _():
        for r in range(x_ref.shape[0] // OUT_PROJ_ROWS):
            rows = pl.ds(r * OUT_PROJ_ROWS, OUT_PROJ_ROWS)
            y = acc_ref[rows, :] + _dot(g_ref[rows, :], w_ref[...])
            x = x_ref[rows, :] + _rms_scale(y, gain_ref[...])
            o_ref[rows, :] = x
            if len(rest) == 2:
                rest[0][rows, :] = _rms_scale(x, next_gain_ref[...]).astype(rest[0].dtype)


def _ffn_down(g, w, layer, gain, next_gain, x, *, tm, tk):
    m, d = x.shape
    f = g.shape[1]
    assert f // tk >= 2
    rows = pl.BlockSpec((tm, d), lambda i, k: (i, 0))
    vec = pl.BlockSpec((1, d), lambda i, k: (0, 0))
    emit_next = next_gain is not None
    out = pl.pallas_call(
        _ffn_down_kernel,
        grid=(m // tm, f // tk),
        in_specs=[
            pl.BlockSpec((tm, tk), lambda i, k: (i, k)),
            pl.BlockSpec((None, tk, d), lambda i, k: (layer, k, 0)),
            vec, vec, rows,
        ],
        out_specs=[rows, rows] if emit_next else [rows],
        out_shape=[jax.ShapeDtypeStruct((m, d), F32)] + ([jax.ShapeDtypeStruct((m, d), BF16)] if emit_next else []),
        scratch_shapes=[pltpu.VMEM((tm, d), F32)],
        compiler_params=_params("parallel", "arbitrary"),
        name="ffn_down_norm_res",
    )(g, w, gain.reshape(1, d), (next_gain if emit_next else gain).reshape(1, d), x)
    return (out[0], out[1]) if emit_next else (out[0], None)


def _tile(extent, preferred):
    t = min(extent, preferred)
    while extent % t:
        t //= 2
    return t


def kernel(x, w_in, w_out, ret_gn_w, swa_sinks, norm_mix_pre, norm_mix_post, norm_ffn_pre, norm_ffn_post, w_up, conv_w, conv_b, w_down):
    batch, seq, d_model = x.shape
    depth = w_in.shape[0]
    d_ff = conv_w.shape[-1]
    assert seq % SB_TILE == 0 and d_ff % 512 == 0
    xf = x.reshape(batch * seq, d_model).astype(F32)
    w_in, w_out, w_up, w_down = (w.astype(BF16) for w in (w_in, w_out, w_up, w_down))
    tm_huge = _tile(seq, 2048)
    tm_big = _tile(seq, 1024)
    tm_small = _tile(seq, 512)
    h = None
    for l in range(depth):
        if h is None:
            proj = _in_proj(xf, norm_mix_pre[l], w_in, l, tm=tm_small, tn=2560)
        else:
            proj = _in_proj(h, None, w_in, l, tm=tm_big, tn=2560)
        o_ret = _retention(proj, ret_gn_w[l], batch, seq, rows=tm_big)
        o_sb = _stick_breaking(proj, batch, seq, rows=tm_big)
        o_swa = _sliding_window(proj, swa_sinks[l], batch, seq, rows=tm_small)
        xf, h = _out_proj(o_ret, o_sb, o_swa, w_out, l, norm_mix_post[l], norm_ffn_pre[l], xf, tm=tm_small)
        g = _ffn_up(h, w_up, l, conv_w[l], conv_b[l], seq, tm=tm_huge, tn=512)
        next_gain = norm_mix_pre[l + 1] if l + 1 < depth else None
        xf, h = _ffn_down(g, w_down, l, norm_ffn_post[l], next_gain, xf, tm=tm_small, tk=d_ff // 2)
    return xf.reshape(batch, seq, d_model).astype(x.dtype)
```

```python
import functools
import math

import jax
import jax.numpy as jnp
import numpy as np
from jax import lax
from jax.experimental import pallas as pl
from jax.experimental.pallas import tpu as pltpu

HEAD_DIM = 128
RET_HEADS = 4
SB_HEADS = 4
SWA_Q_HEADS = 8
SWA_KV_HEADS = 2
SWA_GROUP = SWA_Q_HEADS // SWA_KV_HEADS
RET_CHUNK = 128
WINDOW = 128
CONV_WIDTH = 3
RMS_EPS = 1e-6
GN_EPS = 1e-5

COL_RET_Q, COL_RET_K, COL_RET_V, COL_RET_G = 0, 4, 8, 12
COL_SB_Q, COL_SB_K, COL_SB_V = 16, 20, 24
COL_SWA_Q, COL_SWA_K, COL_SWA_V = 28, 36, 38
D_RET = RET_HEADS * HEAD_DIM
D_SB = SB_HEADS * HEAD_DIM
D_SWA = SWA_Q_HEADS * HEAD_DIM

BF16_SUBLANES = 16
VMEM_LIMIT_BYTES = 56 * 1024 * 1024

SB_LOG2_ZERO = -151.0

F32 = jnp.float32
BF16 = jnp.bfloat16


def _params(*semantics):
    return pltpu.CompilerParams(dimension_semantics=semantics, vmem_limit_bytes=VMEM_LIMIT_BYTES)


def _rms_scale(x, gain):
    ms = jnp.mean(x * x, axis=-1, keepdims=True)
    return x * lax.rsqrt(ms + RMS_EPS) * gain


def _dot(a, b):
    return jnp.dot(a, b, preferred_element_type=F32)


def _dot_nt(a, b):
    return lax.dot_general(a, b, (((1,), (1,)), ((), ())), preferred_element_type=F32)


def _dot_tn(a, b):
    return lax.dot_general(a, b, (((0,), (0,)), ((), ())), preferred_element_type=F32)


NORM_ROWS = 64


def _norm_rows_into(dst_ref, dst_row0, x_ref, gain_ref, rows):
    def body(r, carry):
        src = pl.ds(pl.multiple_of(r * NORM_ROWS, NORM_ROWS), NORM_ROWS)
        dst = pl.ds(pl.multiple_of(dst_row0 + r * NORM_ROWS, BF16_SUBLANES), NORM_ROWS)
        dst_ref[dst, :] = _rms_scale(x_ref[src, :], gain_ref[...]).astype(BF16)
        return carry
    lax.fori_loop(0, rows // NORM_ROWS, body, 0)


def _norm_matmul_kernel(x_ref, gain_ref, w_ref, o_ref, h_ref):
    @pl.when(pl.program_id(1) == 0)
    def _():
        _norm_rows_into(h_ref, 0, x_ref, gain_ref, x_ref.shape[0])
    o_ref[...] = _dot(h_ref[...], w_ref[...]).astype(o_ref.dtype)


def _matmul_kernel(h_ref, w_ref, o_ref):
    o_ref[...] = _dot(h_ref[...], w_ref[...]).astype(o_ref.dtype)


def _in_proj(x, gain, w, layer, *, tm, tn):
    m, d = x.shape
    n = w.shape[2]
    rows = pl.BlockSpec((tm, d), lambda i, j: (i, 0))
    w_spec = pl.BlockSpec((None, d, tn), lambda i, j: (layer, 0, j))
    common = dict(
        grid=(m // tm, n // tn),
        out_specs=pl.BlockSpec((tm, tn), lambda i, j: (i, j)),
        out_shape=jax.ShapeDtypeStruct((m, n), BF16),
        compiler_params=_params("parallel", "arbitrary"),
    )
    if gain is None:
        return pl.pallas_call(_matmul_kernel, in_specs=[rows, w_spec], name="in_proj", **common)(x, w)
    return pl.pallas_call(
        _norm_matmul_kernel,
        in_specs=[rows, pl.BlockSpec((1, d), lambda i, j: (0, 0)), w_spec],
        scratch_shapes=[pltpu.VMEM((tm, d), BF16)],
        name="norm_in_proj",
        **common,
    )(x, gain.reshape(1, d), w)


def _retention_tables():
    c = RET_CHUNK
    h = np.arange(RET_HEADS, dtype=np.float64)
    log_gamma = np.log(1.0 - 2.0 ** (-5.0 - h))
    idx = np.arange(c, dtype=np.float64)
    diff = idx[:, None] - idx[None, :]
    scale = HEAD_DIM ** -0.5
    decay = np.where(diff >= 0, np.exp(log_gamma[:, None, None] * np.maximum(diff, 0.0)), 0.0) * scale
    xi = np.exp(log_gamma[:, None] * (idx + 1.0)[None, :])
    zeta = np.exp(log_gamma[:, None] * (c - 1.0 - idx)[None, :]) * scale
    chunk_decay = np.exp(log_gamma * c)
    ones = np.ones((RET_HEADS, c, HEAD_DIM))
    t = np.stack([decay, xi[:, :, None] * ones, zeta[:, :, None] * ones, chunk_decay[:, None, None] * ones], axis=1)
    return jnp.asarray(t, dtype=F32)


def _retention_kernel(q_ref, k_ref, v_ref, g_ref, tab_ref, gn_ref, o_ref, state_ref):
    @pl.when(pl.program_id(2) == 0)
    def _():
        state_ref[...] = jnp.zeros_like(state_ref)

    c = RET_CHUNK
    decay = tab_ref[0, 0]
    xi = tab_ref[0, 1]
    zeta = tab_ref[0, 2]
    chunk_decay = tab_ref[0, 3]
    gn_w = gn_ref[0]

    chunks = [pl.ds(n * c, c) for n in range(q_ref.shape[0] // c)]
    scores = [(_dot_nt(q_ref[rows, :], k_ref[rows, :]) * decay).astype(BF16) for rows in chunks]
    contribs = [_dot_tn((k_ref[rows, :].astype(F32) * zeta).astype(BF16), v_ref[rows, :]) for rows in chunks]
    states = []
    state = state_ref[...]
    for contrib in contribs:
        states.append(state.astype(BF16))
        state = state * chunk_decay + contrib
    state_ref[...] = state
    outs = [_dot(s, v_ref[rows, :]) + _dot(q_ref[rows, :], st) * xi for rows, s, st in zip(chunks, scores, states)]
    for rows, o in zip(chunks, outs):
        mu = jnp.mean(o, axis=-1, keepdims=True)
        oc = o - mu
        var = jnp.mean(oc * oc, axis=-1, keepdims=True)
        on = oc * lax.rsqrt(var + GN_EPS) * gn_w
        g = g_ref[rows, :].astype(F32)
        o_ref[rows, :] = (g / (1.0 + jnp.exp(-g)) * on).astype(o_ref.dtype)


def _retention(proj, gn_w, batch, seq, *, rows):
    m = proj.shape[0]
    d = HEAD_DIM
    steps = seq // rows

    def col(base):
        return pl.BlockSpec((rows, d), lambda b, h, s: (b * steps + s, base + h))

    return pl.pallas_call(
        _retention_kernel,
        grid=(batch, RET_HEADS, steps),
        in_specs=[
            col(COL_RET_Q), col(COL_RET_K), col(COL_RET_V), col(COL_RET_G),
            pl.BlockSpec((1, 4, RET_CHUNK, d), lambda b, h, s: (h, 0, 0, 0)),
            pl.BlockSpec((1, 1, d), lambda b, h, s: (h, 0, 0)),
        ],
        out_specs=pl.BlockSpec((rows, d), lambda b, h, s: (b * steps + s, h)),
        out_shape=jax.ShapeDtypeStruct((m, D_RET), BF16),
        scratch_shapes=[pltpu.VMEM((d, d), F32)],
        compiler_params=_params("parallel", "parallel", "arbitrary"),
        name="retention",
    )(proj, proj, proj, proj, _retention_tables(), gn_w.reshape(RET_HEADS, 1, d))


SB_TILE = 256


def _sb_kernel(q_ref, k_ref, v_ref, tri_ref, o_ref, acc_ref, carry_ref):
    t = SB_TILE
    to_log2_logit = HEAD_DIM ** -0.5 * math.log2(math.e)
    first_block = pl.program_id(2) * (q_ref.shape[0] // t)

    def log_gates(q, keys):
        z = _dot_nt(q, k_ref[keys, :]) * to_log2_logit
        log_beta = jnp.minimum(z, 0.0) - jnp.log2(1.0 + jnp.exp2(-jnp.abs(z)))
        return log_beta, log_beta - z

    def later_keys_sum(log_keep):
        hi = log_keep.astype(BF16)
        lo = (log_keep - hi.astype(F32)).astype(BF16)
        return _dot(hi, tri_ref[...]) + _dot(lo, tri_ref[...])

    causal = lax.broadcasted_iota(jnp.int32, (t, t), 1) < lax.broadcasted_iota(jnp.int32, (t, t), 0)
    tiles = range(q_ref.shape[0] // t)
    rows = [pl.ds(i * t, t) for i in tiles]
    near = [pl.ds(pl.multiple_of((first_block + i) * t, t), t) for i in tiles]
    far = [pl.ds(pl.multiple_of(jnp.maximum(first_block + i - 1, 0) * t, t), t) for i in tiles]
    has_far = [(first_block + i > 0).astype(F32) for i in tiles]
    gates_near = [log_gates(q_ref[rows[i], :], near[i]) for i in tiles]
    gates_far = [log_gates(q_ref[rows[i], :], far[i]) for i in tiles]
    keep_near = [jnp.where(causal, gates_near[i][1], 0.0) for i in tiles]
    later_near = [later_keys_sum(keep_near[i]) for i in tiles]
    later_far = [later_keys_sum(gates_far[i][1]) for i in tiles]
    carry_near = [jnp.sum(keep_near[i], axis=-1, keepdims=True) for i in tiles]
    a_near = [jnp.where(causal, jnp.exp2(gates_near[i][0] + later_near[i]), 0.0).astype(BF16) for i in tiles]
    a_far = [jnp.exp2(gates_far[i][0] + later_far[i] + carry_near[i]).astype(BF16) for i in tiles]
    for i in tiles:
        v_far = v_ref[far[i], :] * has_far[i].astype(BF16)
        acc_ref[rows[i], :] = _dot(a_near[i], v_ref[near[i], :]) + _dot(a_far[i], v_far)
        carry_ref[rows[i], :] = carry_near[i] + jnp.sum(gates_far[i][1], axis=-1, keepdims=True) * has_far[i]

    @pl.when(jnp.max(carry_ref[...]) > SB_LOG2_ZERO)
    def _():
        def sub_tile(i, carry):
            rows = pl.ds(pl.multiple_of(i * t, t), t)
            q = q_ref[rows, :]

            def cond(state):
                key_block, max_carry = state
                return jnp.logical_and(key_block >= 0, max_carry > SB_LOG2_ZERO)

            def body(state):
                key_block, _ = state
                keys = pl.ds(pl.multiple_of(key_block * t, t), t)
                log_beta, log_keep = log_gates(q, keys)
                a = jnp.exp2(log_beta + later_keys_sum(log_keep) + carry_ref[rows, :])
                acc_ref[rows, :] += _dot(a.astype(BF16), v_ref[keys, :])
                carry_ref[rows, :] += jnp.sum(log_keep, axis=-1, keepdims=True)
                return key_block - 1, jnp.max(carry_ref[rows, :])

            lax.while_loop(cond, body, (first_block + i - 2, jnp.max(carry_ref[rows, :])))
            return carry

        lax.fori_loop(0, q_ref.shape[0] // t, sub_tile, 0)

    o_ref[...] = acc_ref[...].astype(o_ref.dtype)


def _stick_breaking(proj, batch, seq, *, rows):
    m = proj.shape[0]
    d = HEAD_DIM
    t = SB_TILE
    steps = seq // rows
    idx = np.arange(t)
    tri = jnp.asarray(idx[:, None] > idx[None, :], dtype=BF16)
    return pl.pallas_call(
        _sb_kernel,
        grid=(batch, SB_HEADS, steps),
        in_specs=[
            pl.BlockSpec((rows, d), lambda b, h, s: (b * steps + s, COL_SB_Q + h)),
            pl.BlockSpec((seq, d), lambda b, h, s: (b, COL_SB_K + h)),
            pl.BlockSpec((seq, d), lambda b, h, s: (b, COL_SB_V + h)),
            pl.BlockSpec((t, t), lambda b, h, s: (0, 0)),
        ],
        out_specs=pl.BlockSpec((rows, d), lambda b, h, s: (b * steps + s, h)),
        out_shape=jax.ShapeDtypeStruct((m, D_SB), BF16),
        scratch_shapes=[pltpu.VMEM((rows, d), F32), pltpu.VMEM((rows, 1), F32)],
        compiler_params=_params("parallel", "parallel", "arbitrary"),
        name="stick_breaking",
    )(proj, proj, proj, tri)


def _swa_bias():
    w = WINDOW
    qi = np.arange(w)[:, None]
    kj = np.arange(2 * w)[None, :]
    dist = qi + w - kj
    valid = (dist >= 0) & (dist < w)
    slopes = 2.0 ** (-(8.0 / SWA_Q_HEADS) * (np.arange(SWA_Q_HEADS, dtype=np.float64) + 1.0))
    bias = np.where(valid[None], -slopes[:, None, None] * dist[None].astype(np.float64), -np.inf)
    return jnp.asarray(bias, dtype=F32)


def _swa_kernel(sink_ref, q_ref, kc_ref, vc_ref, kp_ref, vp_ref, bias_ref, o_ref, kb_ref, vb_ref):
    w = WINDOW
    d = HEAD_DIM
    scale = d ** -0.5
    rows = q_ref.shape[0]
    kv_head = pl.program_id(1)
    has_prev = pl.program_id(2) > 0
    kb_ref[pl.ds(0, w), :] = kp_ref[...]
    kb_ref[pl.ds(w, rows), :] = kc_ref[...]
    vb_ref[pl.ds(0, w), pl.ds(0, d)] = vp_ref[...]
    vb_ref[pl.ds(w, rows), pl.ds(0, d)] = vc_ref[...]
    vb_ref[:, pl.ds(d, d)] = jnp.ones((rows + w, d), BF16)
    bias = bias_ref[0]
    sinks = [sink_ref[kv_head * SWA_GROUP + g] for g in range(SWA_GROUP)]
    blocks = range(rows // w)
    groups = range(SWA_GROUP)
    heads = [slice(g * w, (g + 1) * w) for g in groups]
    scores = []
    for blk in blocks:
        tok = pl.ds(blk * w, w)
        q = jnp.concatenate([q_ref[tok, pl.ds(g * d, d)] for g in groups], axis=0)
        s = _dot_nt(q, kb_ref[pl.ds(blk * w, 2 * w), :]) * scale + bias
        if blk == 0:
            in_current = lax.broadcasted_iota(jnp.int32, s.shape, 1) >= w
            s = jnp.where(jnp.logical_or(in_current, has_prev), s, -jnp.inf)
        scores.append(s)
    row_max = [jnp.max(s, axis=-1, keepdims=True) for s in scores]
    mx = [[jnp.maximum(row_max[blk][heads[g]], sinks[g]) for g in groups] for blk in blocks]
    probs = [jnp.concatenate([jnp.exp(scores[blk][heads[g]] - mx[blk][g]).astype(BF16) for g in groups], axis=0)
             for blk in blocks]
    outs = [_dot(probs[blk], vb_ref[pl.ds(blk * w, 2 * w), :]) for blk in blocks]
    for blk in blocks:
        for g in groups:
            denom = outs[blk][heads[g], d:] + jnp.exp(sinks[g] - mx[blk][g])
            o_ref[pl.ds(blk * w, w), pl.ds(g * d, d)] = (outs[blk][heads[g], :d] / denom).astype(o_ref.dtype)


def _sliding_window(proj, sinks, batch, seq, *, rows):
    m = proj.shape[0]
    d = HEAD_DIM
    w = WINDOW
    steps = seq // rows
    blocks_per_step = rows // w
    blocks_per_seq = seq // w

    gw = SWA_GROUP * w

    def cur(base):
        return pl.BlockSpec((rows, d), lambda b, h, s, sink: (b * steps + s, base + h))

    def prev(base):
        return pl.BlockSpec(
            (w, d), lambda b, h, s, sink: (b * blocks_per_seq + jnp.maximum(s * blocks_per_step - 1, 0), base + h))

    grid_spec = pltpu.PrefetchScalarGridSpec(
        num_scalar_prefetch=1,
        grid=(batch, SWA_KV_HEADS, steps),
        in_specs=[
            pl.BlockSpec((rows, SWA_GROUP * d), lambda b, h, s, sink: (b * steps + s, COL_SWA_Q // SWA_GROUP + h)),
            cur(COL_SWA_K), cur(COL_SWA_V), prev(COL_SWA_K), prev(COL_SWA_V),
            pl.BlockSpec((1, gw, 2 * w), lambda b, h, s, sink: (h, 0, 0)),
        ],
        out_specs=pl.BlockSpec((rows, SWA_GROUP * d), lambda b, h, s, sink: (b * steps + s, h)),
        scratch_shapes=[pltpu.VMEM((rows + w, d), BF16), pltpu.VMEM((rows + w, 2 * d), BF16)],
    )
    return pl.pallas_call(
        _swa_kernel,
        grid_spec=grid_spec,
        out_shape=jax.ShapeDtypeStruct((m, D_SWA), BF16),
        compiler_params=_params("parallel", "parallel", "arbitrary"),
        name="sliding_window",
    )(sinks.astype(F32), proj, proj, proj, proj, proj, _swa_bias().reshape(SWA_KV_HEADS, gw, 2 * w))


OUT_PROJ_ROWS = 128


def _out_proj_kernel(ret_ref, sb_ref, swa_ref, w_ref, gain_ref, next_gain_ref, x_ref, o_ref, h_ref):
    for r in range(x_ref.shape[0] // OUT_PROJ_ROWS):
        rows = pl.ds(r * OUT_PROJ_ROWS, OUT_PROJ_ROWS)
        y = _dot(ret_ref[rows, :], w_ref[pl.ds(0, D_RET), :])
        y += _dot(sb_ref[rows, :], w_ref[pl.ds(D_RET, D_SB), :])
        y += _dot(swa_ref[rows, :], w_ref[pl.ds(D_RET + D_SB, D_SWA), :])
        x = x_ref[rows, :] + _rms_scale(y, gain_ref[...])
        o_ref[rows, :] = x
        h_ref[rows, :] = _rms_scale(x, next_gain_ref[...]).astype(h_ref.dtype)


def _out_proj(o_ret, o_sb, o_swa, w, layer, gain, next_gain, x, *, tm):
    m, d = x.shape

    def rows(width):
        return pl.BlockSpec((tm, width), lambda i: (i, 0))

    vec = pl.BlockSpec((1, d), lambda i: (0, 0))
    return pl.pallas_call(
        _out_proj_kernel,
        grid=(m // tm,),
        in_specs=[
            rows(D_RET), rows(D_SB), rows(D_SWA),
            pl.BlockSpec((None,) + w.shape[1:], lambda i: (layer, 0, 0)),
            vec, vec, rows(d),
        ],
        out_specs=[rows(d), rows(d)],
        out_shape=[jax.ShapeDtypeStruct((m, d), F32), jax.ShapeDtypeStruct((m, d), BF16)],
        compiler_params=_params("parallel"),
        name="out_proj_norm_res",
    )(o_ret, o_sb, o_swa, w, gain.reshape(1, d), next_gain.reshape(1, d), x)


TAIL = 8
assert TAIL >= CONV_WIDTH - 1
FFN_UP_HALVES = 2
FFN_UP_COLS = 1536
FFN_UP_ROWS = 512


def _gelu_tanh(x):
    return 0.5 * x * (1.0 + jnp.tanh(math.sqrt(2.0 / math.pi) * (x + 0.044715 * (x * x * x))))


def _ffn_up_kernel(h_ref, wa_ref, wb_ref, cw_ref, cb_ref, o_ref, a_ref, b_ref, tail_ref, *, tiles_per_seq):
    tm = h_ref.shape[0]
    starts_sequence = pl.program_id(1) % tiles_per_seq == 0

    @pl.when(starts_sequence)
    def _():
        a_ref[pl.ds(0, TAIL), :] = jnp.zeros((TAIL, a_ref.shape[1]), F32)

    @pl.when(jnp.logical_not(starts_sequence))
    def _():
        a_ref[pl.ds(0, TAIL), :] = tail_ref[...]

    tn = a_ref.shape[1]
    chunks = [pl.ds(c, min(FFN_UP_COLS, tn - c)) for c in range(0, tn, FFN_UP_COLS)]

    def matmuls(cols):
        a_ref[pl.ds(TAIL, tm), cols] = _dot(h_ref[...], wa_ref[:, cols])
        b_ref[:, cols] = _dot(h_ref[...], wb_ref[:, cols])

    def conv_gate(cols):
        for r in range(0, tm, FFN_UP_ROWS):
            a_conv = cb_ref[:, cols]
            for tap in range(CONV_WIDTH):
                taps = pl.ds(r + TAIL - (CONV_WIDTH - 1) + tap, FFN_UP_ROWS)
                a_conv = a_conv + a_ref[taps, cols] * cw_ref[pl.ds(tap, 1), cols]
            rows = pl.ds(r, FFN_UP_ROWS)
            o_ref[rows, cols] = (_gelu_tanh(a_conv) * b_ref[rows, cols]).astype(o_ref.dtype)

    matmuls(chunks[0])
    for c, cols in enumerate(chunks):
        if c + 1 < len(chunks):
            matmuls(chunks[c + 1])
        conv_gate(cols)
    tail_ref[...] = a_ref[pl.ds(tm, TAIL), :]


def _ffn_up(h, w_up, layer, conv_w, conv_b, seq, *, tm):
    m, d = h.shape
    f = conv_w.shape[1]
    tn = f // FFN_UP_HALVES
    resident = dict(pipeline_mode=pl.Buffered(1))
    return pl.pallas_call(
        functools.partial(_ffn_up_kernel, tiles_per_seq=seq // tm),
        grid=(FFN_UP_HALVES, m // tm),
        in_specs=[
            pl.BlockSpec((tm, d), lambda j, i: (i, 0)),
            pl.BlockSpec((None, d, tn), lambda j, i: (layer, 0, j), **resident),
            pl.BlockSpec((None, d, tn), lambda j, i: (layer, 0, FFN_UP_HALVES + j), **resident),
            pl.BlockSpec((CONV_WIDTH, tn), lambda j, i: (0, j)),
            pl.BlockSpec((1, tn), lambda j, i: (0, j)),
        ],
        out_specs=pl.BlockSpec((tm, tn), lambda j, i: (i, j)),
        out_shape=jax.ShapeDtypeStruct((m, f), BF16),
        scratch_shapes=[pltpu.VMEM((TAIL + tm, tn), F32), pltpu.VMEM((tm, tn), F32), pltpu.VMEM((TAIL, tn), F32)],
        compiler_params=_params("arbitrary", "arbitrary"),
        name="ffn_up_conv_gate",
    )(h, w_up, w_up, conv_w, conv_b.reshape(1, f))


def _ffn_down_kernel(g_ref, w_ref, gain_ref, next_gain_ref, x_ref, o_ref, *maybe_h_ref):
    for r in range(x_ref.shape[0] // OUT_PROJ_ROWS):
        rows = pl.ds(r * OUT_PROJ_ROWS, OUT_PROJ_ROWS)
        x = x_ref[rows, :] + _rms_scale(_dot(g_ref[rows, :], w_ref[...]), gain_ref[...])
        o_ref[rows, :] = x
        for h_ref in maybe_h_ref:
            h_ref[rows, :] = _rms_scale(x, next_gain_ref[...]).astype(h_ref.dtype)


def _ffn_down(g, w, layer, gain, next_gain, x, *, tm):
    m, d = x.shape
    f = g.shape[1]
    rows = pl.BlockSpec((tm, d), lambda i: (i, 0))
    vec = pl.BlockSpec((1, d), lambda i: (0, 0))
    emit_next = next_gain is not None
    out = pl.pallas_call(
        _ffn_down_kernel,
        grid=(m // tm,),
        in_specs=[
            pl.BlockSpec((tm, f), lambda i: (i, 0)),
            pl.BlockSpec((None, f, d), lambda i: (layer, 0, 0), pipeline_mode=pl.Buffered(1)),
            vec, vec, rows,
        ],
        out_specs=[rows, rows] if emit_next else [rows],
        out_shape=[jax.ShapeDtypeStruct((m, d), F32)] + ([jax.ShapeDtypeStruct((m, d), BF16)] if emit_next else []),
        compiler_params=_params("parallel"),
        name="ffn_down_norm_res",
    )(g, w, gain.reshape(1, d), (next_gain if emit_next else gain).reshape(1, d), x)
    return (out[0], out[1]) if emit_next else (out[0], None)


def _tile(extent, preferred):
    t = min(extent, preferred)
    while extent % t:
        t //= 2
    return t


def kernel(x, w_in, w_out, ret_gn_w, swa_sinks, norm_mix_pre, norm_mix_post, norm_ffn_pre, norm_ffn_post, w_up, conv_w, conv_b, w_down):
    batch, seq, d_model = x.shape
    depth = w_in.shape[0]
    d_ff = conv_w.shape[-1]
    assert seq % SB_TILE == 0 and d_ff % 512 == 0
    xf = x.reshape(batch * seq, d_model).astype(F32)
    w_in, w_out, w_up, w_down = (w.astype(BF16) for w in (w_in, w_out, w_up, w_down))
    tm_huge = _tile(seq, 2048)
    tm_big = _tile(seq, 1024)
    tm_small = _tile(seq, 512)
    h = None
    for l in range(depth):
        if h is None:
            proj = _in_proj(xf, norm_mix_pre[l], w_in, l, tm=tm_small, tn=2560)
        else:
            proj = _in_proj(h, None, w_in, l, tm=tm_big, tn=2560)
        o_ret = _retention(proj, ret_gn_w[l], batch, seq, rows=tm_big)
        o_sb = _stick_breaking(proj, batch, seq, rows=tm_big)
        o_swa = _sliding_window(proj, swa_sinks[l], batch, seq, rows=tm_small)
        xf, h = _out_proj(o_ret, o_sb, o_swa, w_out, l, norm_mix_post[l], norm_ffn_pre[l], xf, tm=tm_small)
        g = _ffn_up(h, w_up, l, conv_w[l], conv_b[l], seq, tm=tm_small)
        next_gain = norm_mix_pre[l + 1] if l + 1 < depth else None
        xf, h = _ffn_down(g, w_down, l, norm_ffn_post[l], next_gain, xf, tm=tm_small)
    return xf.reshape(batch, seq, d_model).astype(x.dtype)
```

```python
import functools
import math

import jax
import jax.numpy as jnp
import numpy as np
from jax import lax
from jax.experimental import pallas as pl
from jax.experimental.pallas import tpu as pltpu

HEAD_DIM = 128
RET_HEADS = 4
SB_HEADS = 4
SWA_Q_HEADS = 8
SWA_KV_HEADS = 2
SWA_GROUP = SWA_Q_HEADS // SWA_KV_HEADS
RET_CHUNK = 128
WINDOW = 128
CONV_WIDTH = 3
RMS_EPS = 1e-6
GN_EPS = 1e-5

COL_RET_Q, COL_RET_K, COL_RET_V, COL_RET_G = 0, 4, 8, 12
COL_SB_Q, COL_SB_K, COL_SB_V = 16, 20, 24
COL_SWA_Q, COL_SWA_K, COL_SWA_V = 28, 36, 38
D_RET = RET_HEADS * HEAD_DIM
D_SB = SB_HEADS * HEAD_DIM
D_SWA = SWA_Q_HEADS * HEAD_DIM

VMEM_LIMIT_BYTES = 56 * 1024 * 1024

SB_LOG2_ZERO = -151.0

F32 = jnp.float32
BF16 = jnp.bfloat16


def _params(*semantics):
    return pltpu.CompilerParams(dimension_semantics=semantics, vmem_limit_bytes=VMEM_LIMIT_BYTES)


def _rms_scale(x, gain):
    ms = jnp.mean(x * x, axis=-1, keepdims=True)
    return x * lax.rsqrt(ms + RMS_EPS) * gain


def _dot(a, b):
    return jnp.dot(a, b, preferred_element_type=F32)


def _dot_nt(a, b):
    return lax.dot_general(a, b, (((1,), (1,)), ((), ())), preferred_element_type=F32)


def _dot_tn(a, b):
    return lax.dot_general(a, b, (((0,), (0,)), ((), ())), preferred_element_type=F32)


IN_PROJ_NORM_ROWS = 128


def _norm_matmul_kernel(x_ref, gain_ref, w_ref, o_ref):
    for r in range(0, x_ref.shape[0], IN_PROJ_NORM_ROWS):
        rows = pl.ds(r, IN_PROJ_NORM_ROWS)
        h = _rms_scale(x_ref[rows, :], gain_ref[...]).astype(BF16)
        o_ref[rows, :] = _dot(h, w_ref[...]).astype(o_ref.dtype)


def _matmul_kernel(h_ref, w_ref, o_ref):
    o_ref[...] = _dot(h_ref[...], w_ref[...]).astype(o_ref.dtype)


def _in_proj(x, gain, w, layer, *, tm):
    m, d = x.shape
    n = w.shape[2]
    rows = pl.BlockSpec((tm, d), lambda i: (i, 0))
    w_spec = pl.BlockSpec((None, d, n), lambda i: (layer, 0, 0), pipeline_mode=pl.Buffered(1))
    common = dict(
        grid=(m // tm,),
        out_specs=pl.BlockSpec((tm, n), lambda i: (i, 0)),
        out_shape=jax.ShapeDtypeStruct((m, n), BF16),
        compiler_params=_params("parallel"),
    )
    if gain is None:
        return pl.pallas_call(_matmul_kernel, in_specs=[rows, w_spec], name="in_proj", **common)(x, w)
    return pl.pallas_call(
        _norm_matmul_kernel,
        in_specs=[rows, pl.BlockSpec((1, d), lambda i: (0, 0)), w_spec],
        name="norm_in_proj",
        **common,
    )(x, gain.reshape(1, d), w)


def _retention_tables():
    c = RET_CHUNK
    h = np.arange(RET_HEADS, dtype=np.float64)
    log_gamma = np.log(1.0 - 2.0 ** (-5.0 - h))
    idx = np.arange(c, dtype=np.float64)
    diff = idx[:, None] - idx[None, :]
    scale = HEAD_DIM ** -0.5
    decay = np.where(diff >= 0, np.exp(log_gamma[:, None, None] * np.maximum(diff, 0.0)), 0.0) * scale
    xi = np.exp(log_gamma[:, None] * (idx + 1.0)[None, :])
    zeta = np.exp(log_gamma[:, None] * (c - 1.0 - idx)[None, :]) * scale
    chunk_decay = np.exp(log_gamma * c)
    ones = np.ones((RET_HEADS, c, HEAD_DIM))
    t = np.stack([decay, xi[:, :, None] * ones, zeta[:, :, None] * ones, chunk_decay[:, None, None] * ones], axis=1)
    return jnp.asarray(t, dtype=F32)


def _retention_kernel(q_ref, k_ref, v_ref, g_ref, tab_ref, gn_ref, o_ref, state_ref):
    @pl.when(pl.program_id(2) == 0)
    def _():
        state_ref[...] = jnp.zeros_like(state_ref)

    c = RET_CHUNK
    decay = tab_ref[0, 0]
    xi = tab_ref[0, 1]
    zeta = tab_ref[0, 2]
    chunk_decay = tab_ref[0, 3]
    gn_w = gn_ref[0]

    chunks = [pl.ds(n * c, c) for n in range(q_ref.shape[0] // c)]
    scores = [(_dot_nt(q_ref[rows, :], k_ref[rows, :]) * decay).astype(BF16) for rows in chunks]
    contribs = [_dot_tn((k_ref[rows, :].astype(F32) * zeta).astype(BF16), v_ref[rows, :]) for rows in chunks]
    states = []
    state = state_ref[...]
    for contrib in contribs:
        states.append(state.astype(BF16))
        state = state * chunk_decay + contrib
    state_ref[...] = state
    outs = [_dot(s, v_ref[rows, :]) + _dot(q_ref[rows, :], st) * xi for rows, s, st in zip(chunks, scores, states)]
    for rows, o in zip(chunks, outs):
        mu = jnp.mean(o, axis=-1, keepdims=True)
        oc = o - mu
        var = jnp.mean(oc * oc, axis=-1, keepdims=True)
        on = oc * lax.rsqrt(var + GN_EPS) * gn_w
        g = g_ref[rows, :].astype(F32)
        o_ref[rows, :] = (g / (1.0 + jnp.exp(-g)) * on).astype(o_ref.dtype)


def _retention(proj, gn_w, batch, seq, *, rows):
    m = proj.shape[0]
    d = HEAD_DIM
    steps = seq // rows

    def col(base):
        return pl.BlockSpec((rows, d), lambda b, h, s: (b * steps + s, base + h))

    return pl.pallas_call(
        _retention_kernel,
        grid=(batch, RET_HEADS, steps),
        in_specs=[
            col(COL_RET_Q), col(COL_RET_K), col(COL_RET_V), col(COL_RET_G),
            pl.BlockSpec((1, 4, RET_CHUNK, d), lambda b, h, s: (h, 0, 0, 0)),
            pl.BlockSpec((1, 1, d), lambda b, h, s: (h, 0, 0)),
        ],
        out_specs=pl.BlockSpec((rows, d), lambda b, h, s: (b * steps + s, h)),
        out_shape=jax.ShapeDtypeStruct((m, D_RET), BF16),
        scratch_shapes=[pltpu.VMEM((d, d), F32)],
        compiler_params=_params("parallel", "parallel", "arbitrary"),
        name="retention",
    )(proj, proj, proj, proj, _retention_tables(), gn_w.reshape(RET_HEADS, 1, d))


SB_TILE = 256


def _sb_kernel(q_ref, k_ref, v_ref, tri_ref, o_ref, acc_ref, carry_ref):
    t = SB_TILE
    to_log2_logit = HEAD_DIM ** -0.5 * math.log2(math.e)
    first_block = pl.program_id(2) * (q_ref.shape[0] // t)

    def log_gates(q, keys):
        z = _dot_nt(q, k_ref[keys, :]) * to_log2_logit
        log_beta = jnp.minimum(z, 0.0) - jnp.log2(1.0 + jnp.exp2(-jnp.abs(z)))
        return log_beta, log_beta - z

    def later_keys_sum(log_keep):
        return _dot(log_keep.astype(BF16), tri_ref[...])

    causal = lax.broadcasted_iota(jnp.int32, (t, t), 1) < lax.broadcasted_iota(jnp.int32, (t, t), 0)
    tiles = range(q_ref.shape[0] // t)
    rows = [pl.ds(i * t, t) for i in tiles]
    near = [pl.ds(pl.multiple_of((first_block + i) * t, t), t) for i in tiles]
    far = [pl.ds(pl.multiple_of(jnp.maximum(first_block + i - 1, 0) * t, t), t) for i in tiles]
    has_far = [(first_block + i > 0).astype(F32) for i in tiles]
    gates_near = [log_gates(q_ref[rows[i], :], near[i]) for i in tiles]
    gates_far = [log_gates(q_ref[rows[i], :], far[i]) for i in tiles]
    keep_near = [jnp.where(causal, gates_near[i][1], 0.0) for i in tiles]
    later_near = [later_keys_sum(keep_near[i]) for i in tiles]
    later_far = [later_keys_sum(gates_far[i][1]) for i in tiles]
    carry_near = [jnp.sum(keep_near[i], axis=-1, keepdims=True) for i in tiles]
    a_near = [jnp.where(causal, jnp.exp2(gates_near[i][0] + later_near[i]), 0.0).astype(BF16) for i in tiles]
    a_far = [jnp.exp2(gates_far[i][0] + later_far[i] + carry_near[i]).astype(BF16) for i in tiles]
    for i in tiles:
        v_far = v_ref[far[i], :] * has_far[i].astype(BF16)
        acc_ref[rows[i], :] = _dot(a_near[i], v_ref[near[i], :]) + _dot(a_far[i], v_far)
        carry_ref[rows[i], :] = carry_near[i] + jnp.sum(gates_far[i][1], axis=-1, keepdims=True) * has_far[i]

    @pl.when(jnp.max(carry_ref[...]) > SB_LOG2_ZERO)
    def _():
        def sub_tile(i, carry):
            rows = pl.ds(pl.multiple_of(i * t, t), t)
            q = q_ref[rows, :]

            def cond(state):
                key_block, max_carry = state
                return jnp.logical_and(key_block >= 0, max_carry > SB_LOG2_ZERO)

            def body(state):
                key_block, _ = state
                keys = pl.ds(pl.multiple_of(key_block * t, t), t)
                log_beta, log_keep = log_gates(q, keys)
                a = jnp.exp2(log_beta + later_keys_sum(log_keep) + carry_ref[rows, :])
                acc_ref[rows, :] += _dot(a.astype(BF16), v_ref[keys, :])
                carry_ref[rows, :] += jnp.sum(log_keep, axis=-1, keepdims=True)
                return key_block - 1, jnp.max(carry_ref[rows, :])

            lax.while_loop(cond, body, (first_block + i - 2, jnp.max(carry_ref[rows, :])))
            return carry

        lax.fori_loop(0, q_ref.shape[0] // t, sub_tile, 0)

    o_ref[...] = acc_ref[...].astype(o_ref.dtype)


def _stick_breaking(proj, batch, seq, *, rows):
    m = proj.shape[0]
    d = HEAD_DIM
    t = SB_TILE
    steps = seq // rows
    idx = np.arange(t)
    tri = jnp.asarray(idx[:, None] > idx[None, :], dtype=BF16)
    return pl.pallas_call(
        _sb_kernel,
        grid=(batch, SB_HEADS, steps),
        in_specs=[
            pl.BlockSpec((rows, d), lambda b, h, s: (b * steps + s, COL_SB_Q + h)),
            pl.BlockSpec((seq, d), lambda b, h, s: (b, COL_SB_K + h)),
            pl.BlockSpec((seq, d), lambda b, h, s: (b, COL_SB_V + h)),
            pl.BlockSpec((t, t), lambda b, h, s: (0, 0)),
        ],
        out_specs=pl.BlockSpec((rows, d), lambda b, h, s: (b * steps + s, h)),
        out_shape=jax.ShapeDtypeStruct((m, D_SB), BF16),
        scratch_shapes=[pltpu.VMEM((rows, d), F32), pltpu.VMEM((rows, 1), F32)],
        compiler_params=_params("parallel", "parallel", "arbitrary"),
        name="stick_breaking",
    )(proj, proj, proj, tri)


def _swa_bias():
    w = WINDOW
    qi = np.arange(w)[:, None]
    kj = np.arange(2 * w)[None, :]
    dist = qi + w - kj
    valid = (dist >= 0) & (dist < w)
    slopes = 2.0 ** (-(8.0 / SWA_Q_HEADS) * (np.arange(SWA_Q_HEADS, dtype=np.float64) + 1.0))
    bias = np.where(valid[None], -slopes[:, None, None] * dist[None].astype(np.float64), -np.inf)
    return jnp.asarray(bias, dtype=F32)


def _swa_kernel(sink_ref, q_ref, kc_ref, vc_ref, kp_ref, vp_ref, bias_ref, o_ref, kb_ref, vb_ref):
    w = WINDOW
    d = HEAD_DIM
    scale = d ** -0.5
    rows = q_ref.shape[0]
    kv_head = pl.program_id(1)
    has_prev = pl.program_id(2) > 0
    kb_ref[pl.ds(0, w), :] = kp_ref[...]
    kb_ref[pl.ds(w, rows), :] = kc_ref[...]
    vb_ref[pl.ds(0, w), pl.ds(0, d)] = vp_ref[...]
    vb_ref[pl.ds(w, rows), pl.ds(0, d)] = vc_ref[...]
    vb_ref[:, pl.ds(d, d)] = jnp.ones((rows + w, d), BF16)
    bias = bias_ref[0]
    sinks = [sink_ref[kv_head * SWA_GROUP + g] for g in range(SWA_GROUP)]
    blocks = range(rows // w)
    groups = range(SWA_GROUP)
    heads = [slice(g * w, (g + 1) * w) for g in groups]
    scores = []
    for blk in blocks:
        tok = pl.ds(blk * w, w)
        q = jnp.concatenate([q_ref[tok, pl.ds(g * d, d)] for g in groups], axis=0)
        s = _dot_nt(q, kb_ref[pl.ds(blk * w, 2 * w), :]) * scale + bias
        if blk == 0:
            in_current = lax.broadcasted_iota(jnp.int32, s.shape, 1) >= w
            s = jnp.where(jnp.logical_or(in_current, has_prev), s, -jnp.inf)
        scores.append(s)
    row_max = [jnp.max(s, axis=-1, keepdims=True) for s in scores]
    mx = [[jnp.maximum(row_max[blk][heads[g]], sinks[g]) for g in groups] for blk in blocks]
    probs = [jnp.concatenate([jnp.exp(scores[blk][heads[g]] - mx[blk][g]).astype(BF16) for g in groups], axis=0)
             for blk in blocks]
    outs = [_dot(probs[blk], vb_ref[pl.ds(blk * w, 2 * w), :]) for blk in blocks]
    for blk in blocks:
        for g in groups:
            denom = outs[blk][heads[g], d:] + jnp.exp(sinks[g] - mx[blk][g])
            o_ref[pl.ds(blk * w, w), pl.ds(g * d, d)] = (outs[blk][heads[g], :d] / denom).astype(o_ref.dtype)


def _sliding_window(proj, sinks, batch, seq, *, rows):
    m = proj.shape[0]
    d = HEAD_DIM
    w = WINDOW
    steps = seq // rows
    blocks_per_step = rows // w
    blocks_per_seq = seq // w

    gw = SWA_GROUP * w

    def cur(base):
        return pl.BlockSpec((rows, d), lambda b, h, s, sink: (b * steps + s, base + h))

    def prev(base):
        return pl.BlockSpec(
            (w, d), lambda b, h, s, sink: (b * blocks_per_seq + jnp.maximum(s * blocks_per_step - 1, 0), base + h))

    grid_spec = pltpu.PrefetchScalarGridSpec(
        num_scalar_prefetch=1,
        grid=(batch, SWA_KV_HEADS, steps),
        in_specs=[
            pl.BlockSpec((rows, SWA_GROUP * d), lambda b, h, s, sink: (b * steps + s, COL_SWA_Q // SWA_GROUP + h)),
            cur(COL_SWA_K), cur(COL_SWA_V), prev(COL_SWA_K), prev(COL_SWA_V),
            pl.BlockSpec((1, gw, 2 * w), lambda b, h, s, sink: (h, 0, 0)),
        ],
        out_specs=pl.BlockSpec((rows, SWA_GROUP * d), lambda b, h, s, sink: (b * steps + s, h)),
        scratch_shapes=[pltpu.VMEM((rows + w, d), BF16), pltpu.VMEM((rows + w, 2 * d), BF16)],
    )
    return pl.pallas_call(
        _swa_kernel,
        grid_spec=grid_spec,
        out_shape=jax.ShapeDtypeStruct((m, D_SWA), BF16),
        compiler_params=_params("parallel", "parallel", "arbitrary"),
        name="sliding_window",
    )(sinks.astype(F32), proj, proj, proj, proj, proj, _swa_bias().reshape(SWA_KV_HEADS, gw, 2 * w))


OUT_PROJ_ROWS = 128


def _out_proj_kernel(ret_ref, sb_ref, swa_ref, w_ref, gain_ref, next_gain_ref, x_ref, o_ref, h_ref):
    for r in range(x_ref.shape[0] // OUT_PROJ_ROWS):
        rows = pl.ds(r * OUT_PROJ_ROWS, OUT_PROJ_ROWS)
        y = _dot(ret_ref[rows, :], w_ref[pl.ds(0, D_RET), :])
        y += _dot(sb_ref[rows, :], w_ref[pl.ds(D_RET, D_SB), :])
        y += _dot(swa_ref[rows, :], w_ref[pl.ds(D_RET + D_SB, D_SWA), :])
        x = x_ref[rows, :] + _rms_scale(y, gain_ref[...])
        o_ref[rows, :] = x
        h_ref[rows, :] = _rms_scale(x, next_gain_ref[...]).astype(h_ref.dtype)


def _out_proj(o_ret, o_sb, o_swa, w, layer, gain, next_gain, x, *, tm):
    m, d = x.shape

    def rows(width):
        return pl.BlockSpec((tm, width), lambda i: (i, 0))

    vec = pl.BlockSpec((1, d), lambda i: (0, 0))
    return pl.pallas_call(
        _out_proj_kernel,
        grid=(m // tm,),
        in_specs=[
            rows(D_RET), rows(D_SB), rows(D_SWA),
            pl.BlockSpec((None,) + w.shape[1:], lambda i: (layer, 0, 0)),
            vec, vec, rows(d),
        ],
        out_specs=[rows(d), rows(d)],
        out_shape=[jax.ShapeDtypeStruct((m, d), F32), jax.ShapeDtypeStruct((m, d), BF16)],
        compiler_params=_params("parallel"),
        name="out_proj_norm_res",
    )(o_ret, o_sb, o_swa, w, gain.reshape(1, d), next_gain.reshape(1, d), x)


TAIL = 8
assert TAIL >= CONV_WIDTH - 1
FFN_UP_HALVES = 2
FFN_UP_CHUNKS = (1536, 1280)
FFN_UP_ROWS = 512


def _gelu_tanh(x):
    return 0.5 * x * (1.0 + jnp.tanh(math.sqrt(2.0 / math.pi) * (x + 0.044715 * (x * x * x))))


def _ffn_up_kernel(h_ref, wa_ref, wb_ref, cw_ref, cb_ref, o_ref, a_ref, b_ref, tail_ref, *, tiles_per_seq):
    tm = h_ref.shape[0]
    starts_sequence = pl.program_id(1) % tiles_per_seq == 0

    @pl.when(starts_sequence)
    def _():
        a_ref[pl.ds(0, TAIL), :] = jnp.zeros((TAIL, a_ref.shape[1]), F32)

    @pl.when(jnp.logical_not(starts_sequence))
    def _():
        a_ref[pl.ds(0, TAIL), :] = tail_ref[...]

    tn = a_ref.shape[1]
    assert sum(FFN_UP_CHUNKS) == tn
    chunks = [pl.ds(sum(FFN_UP_CHUNKS[:c]), width) for c, width in enumerate(FFN_UP_CHUNKS)]

    def matmuls(cols):
        a_ref[pl.ds(TAIL, tm), cols] = _dot(h_ref[...], wa_ref[:, cols])
        b_ref[:, cols] = _dot(h_ref[...], wb_ref[:, cols])

    def conv_gate(cols):
        for r in range(0, tm, FFN_UP_ROWS):
            a_conv = cb_ref[:, cols]
            for tap in range(CONV_WIDTH):
                taps = pl.ds(r + TAIL - (CONV_WIDTH - 1) + tap, FFN_UP_ROWS)
                a_conv = a_conv + a_ref[taps, cols] * cw_ref[pl.ds(tap, 1), cols]
            rows = pl.ds(r, FFN_UP_ROWS)
            o_ref[rows, cols] = (_gelu_tanh(a_conv) * b_ref[rows, cols]).astype(o_ref.dtype)

    matmuls(chunks[0])
    for c, cols in enumerate(chunks):
        if c + 1 < len(chunks):
            matmuls(chunks[c + 1])
        conv_gate(cols)
    tail_ref[...] = a_ref[pl.ds(tm, TAIL), :]


def _ffn_up(h, w_up, layer, conv_w, conv_b, seq, *, tm):
    m, d = h.shape
    f = conv_w.shape[1]
    tn = f // FFN_UP_HALVES
    resident = dict(pipeline_mode=pl.Buffered(1))
    return pl.pallas_call(
        functools.partial(_ffn_up_kernel, tiles_per_seq=seq // tm),
        grid=(FFN_UP_HALVES, m // tm),
        in_specs=[
            pl.BlockSpec((tm, d), lambda j, i: (i, 0)),
            pl.BlockSpec((None, d, tn), lambda j, i: (layer, 0, j), **resident),
            pl.BlockSpec((None, d, tn), lambda j, i: (layer, 0, FFN_UP_HALVES + j), **resident),
            pl.BlockSpec((CONV_WIDTH, tn), lambda j, i: (0, j)),
            pl.BlockSpec((1, tn), lambda j, i: (0, j)),
        ],
        out_specs=pl.BlockSpec((tm, tn), lambda j, i: (i, j)),
        out_shape=jax.ShapeDtypeStruct((m, f), BF16),
        scratch_shapes=[pltpu.VMEM((TAIL + tm, tn), F32), pltpu.VMEM((tm, tn), F32), pltpu.VMEM((TAIL, tn), F32)],
        compiler_params=_params("arbitrary", "arbitrary"),
        name="ffn_up_conv_gate",
    )(h, w_up, w_up, conv_w, conv_b.reshape(1, f))


def _ffn_down_kernel(g_ref, w_ref, gain_ref, next_gain_ref, x_ref, o_ref, *maybe_h_ref):
    for r in range(x_ref.shape[0] // OUT_PROJ_ROWS):
        rows = pl.ds(r * OUT_PROJ_ROWS, OUT_PROJ_ROWS)
        x = x_ref[rows, :] + _rms_scale(_dot(g_ref[rows, :], w_ref[...]), gain_ref[...])
        o_ref[rows, :] = x
        for h_ref in maybe_h_ref:
            h_ref[rows, :] = _rms_scale(x, next_gain_ref[...]).astype(h_ref.dtype)


def _ffn_down(g, w, layer, gain, next_gain, x, *, tm):
    m, d = x.shape
    f = g.shape[1]
    rows = pl.BlockSpec((tm, d), lambda i: (i, 0))
    vec = pl.BlockSpec((1, d), lambda i: (0, 0))
    emit_next = next_gain is not None
    out = pl.pallas_call(
        _ffn_down_kernel,
        grid=(m // tm,),
        in_specs=[
            pl.BlockSpec((tm, f), lambda i: (i, 0)),
            pl.BlockSpec((None, f, d), lambda i: (layer, 0, 0), pipeline_mode=pl.Buffered(1)),
            vec, vec, rows,
        ],
        out_specs=[rows, rows] if emit_next else [rows],
        out_shape=[jax.ShapeDtypeStruct((m, d), F32)] + ([jax.ShapeDtypeStruct((m, d), BF16)] if emit_next else []),
        compiler_params=_params("parallel"),
        name="ffn_down_norm_res",
    )(g, w, gain.reshape(1, d), (next_gain if emit_next else gain).reshape(1, d), x)
    return (out[0], out[1]) if emit_next else (out[0], None)


def _tile(extent, preferred):
    t = min(extent, preferred)
    while extent % t:
        t //= 2
    return t


def kernel(x, w_in, w_out, ret_gn_w, swa_sinks, norm_mix_pre, norm_mix_post, norm_ffn_pre, norm_ffn_post, w_up, conv_w, conv_b, w_down):
    batch, seq, d_model = x.shape
    depth = w_in.shape[0]
    d_ff = conv_w.shape[-1]
    assert seq % SB_TILE == 0 and d_ff % 512 == 0
    xf = x.reshape(batch * seq, d_model).astype(F32)
    w_in, w_out, w_up, w_down = (w.astype(BF16) for w in (w_in, w_out, w_up, w_down))
    tm_big = _tile(seq, 1024)
    tm_small = _tile(seq, 512)
    h = None
    for l in range(depth):
        if h is None:
            proj = _in_proj(xf, norm_mix_pre[l], w_in, l, tm=tm_small)
        else:
            proj = _in_proj(h, None, w_in, l, tm=tm_small)
        o_ret = _retention(proj, ret_gn_w[l], batch, seq, rows=tm_big)
        o_sb = _stick_breaking(proj, batch, seq, rows=tm_big)
        o_swa = _sliding_window(proj, swa_sinks[l], batch, seq, rows=tm_small)
        xf, h = _out_proj(o_ret, o_sb, o_swa, w_out, l, norm_mix_post[l], norm_ffn_pre[l], xf, tm=tm_small)
        g = _ffn_up(h, w_up, l, conv_w[l], conv_b[l], seq, tm=tm_small)
        next_gain = norm_mix_pre[l + 1] if l + 1 < depth else None
        xf, h = _ffn_down(g, w_down, l, norm_ffn_post[l], next_gain, xf, tm=tm_small)
    return xf.reshape(batch, seq, d_model).astype(x.dtype)
```

```python
import functools
import math

import jax
import jax.numpy as jnp
import numpy as np
from jax import lax
from jax.experimental import pallas as pl
from jax.experimental.pallas import tpu as pltpu

HEAD_DIM = 128
RET_HEADS = 4
SB_HEADS = 4
SWA_Q_HEADS = 8
SWA_KV_HEADS = 2
SWA_GROUP = SWA_Q_HEADS // SWA_KV_HEADS
RET_CHUNK = 128
WINDOW = 128
CONV_WIDTH = 3
RMS_EPS = 1e-6
GN_EPS = 1e-5

COL_RET_Q, COL_RET_K, COL_RET_V, COL_RET_G = 0, 4, 8, 12
COL_SB_Q, COL_SB_K, COL_SB_V = 16, 20, 24
COL_SWA_Q, COL_SWA_K, COL_SWA_V = 28, 36, 38
D_RET = RET_HEADS * HEAD_DIM
D_SB = SB_HEADS * HEAD_DIM
D_SWA = SWA_Q_HEADS * HEAD_DIM

VMEM_LIMIT_BYTES = 56 * 1024 * 1024

SB_LOG2_ZERO = -151.0

F32 = jnp.float32
BF16 = jnp.bfloat16


def _params(*semantics):
    return pltpu.CompilerParams(dimension_semantics=semantics, vmem_limit_bytes=VMEM_LIMIT_BYTES)


def _rms_scale(x, gain):
    ms = jnp.mean(x * x, axis=-1, keepdims=True)
    return x * lax.rsqrt(ms + RMS_EPS) * gain


def _dot(a, b):
    return jnp.dot(a, b, preferred_element_type=F32)


def _dot_nt(a, b):
    return lax.dot_general(a, b, (((1,), (1,)), ((), ())), preferred_element_type=F32)


def _dot_tn(a, b):
    return lax.dot_general(a, b, (((0,), (0,)), ((), ())), preferred_element_type=F32)


IN_PROJ_NORM_ROWS = 128


def _norm_matmul_kernel(x_ref, gain_ref, w_ref, o_ref):
    for r in range(0, x_ref.shape[0], IN_PROJ_NORM_ROWS):
        rows = pl.ds(r, IN_PROJ_NORM_ROWS)
        h = _rms_scale(x_ref[rows, :], gain_ref[...]).astype(BF16)
        o_ref[rows, :] = _dot(h, w_ref[...]).astype(o_ref.dtype)


def _matmul_kernel(h_ref, w_ref, o_ref):
    o_ref[...] = _dot(h_ref[...], w_ref[...]).astype(o_ref.dtype)


def _in_proj(x, gain, w, layer, *, tm):
    m, d = x.shape
    n = w.shape[2]
    rows = pl.BlockSpec((tm, d), lambda i: (i, 0))
    w_spec = pl.BlockSpec((None, d, n), lambda i: (layer, 0, 0), pipeline_mode=pl.Buffered(1))
    common = dict(
        grid=(m // tm,),
        out_specs=pl.BlockSpec((tm, n), lambda i: (i, 0)),
        out_shape=jax.ShapeDtypeStruct((m, n), BF16),
        compiler_params=_params("parallel"),
    )
    if gain is None:
        return pl.pallas_call(_matmul_kernel, in_specs=[rows, w_spec], name="in_proj", **common)(x, w)
    return pl.pallas_call(
        _norm_matmul_kernel,
        in_specs=[rows, pl.BlockSpec((1, d), lambda i: (0, 0)), w_spec],
        name="norm_in_proj",
        **common,
    )(x, gain.reshape(1, d), w)


def _retention_tables():
    c = RET_CHUNK
    h = np.arange(RET_HEADS, dtype=np.float64)
    log_gamma = np.log(1.0 - 2.0 ** (-5.0 - h))
    idx = np.arange(c, dtype=np.float64)
    diff = idx[:, None] - idx[None, :]
    scale = HEAD_DIM ** -0.5
    decay = np.where(diff >= 0, np.exp(log_gamma[:, None, None] * np.maximum(diff, 0.0)), 0.0) * scale
    xi = np.exp(log_gamma[:, None] * (idx + 1.0)[None, :])
    zeta = np.exp(log_gamma[:, None] * (c - 1.0 - idx)[None, :]) * scale
    chunk_decay = np.exp(log_gamma * c)
    ones = np.ones((RET_HEADS, c, HEAD_DIM))
    t = np.stack([decay, xi[:, :, None] * ones, zeta[:, :, None] * ones, chunk_decay[:, None, None] * ones], axis=1)
    return jnp.asarray(t, dtype=F32)


def _retention_kernel(q_ref, k_ref, v_ref, g_ref, tab_ref, gn_ref, o_ref, state_ref):
    @pl.when(pl.program_id(2) == 0)
    def _():
        state_ref[...] = jnp.zeros_like(state_ref)

    c = RET_CHUNK
    decay = tab_ref[0, 0]
    xi = tab_ref[0, 1]
    zeta = tab_ref[0, 2]
    chunk_decay = tab_ref[0, 3]
    gn_w = gn_ref[0]

    chunks = [pl.ds(n * c, c) for n in range(q_ref.shape[0] // c)]
    scores = [(_dot_nt(q_ref[rows, :], k_ref[rows, :]) * decay).astype(BF16) for rows in chunks]
    contribs = [_dot_tn((k_ref[rows, :].astype(F32) * zeta).astype(BF16), v_ref[rows, :]) for rows in chunks]
    states = []
    state = state_ref[...]
    for contrib in contribs:
        states.append(state.astype(BF16))
        state = state * chunk_decay + contrib
    state_ref[...] = state
    outs = [_dot(s, v_ref[rows, :]) + _dot(q_ref[rows, :], st) * xi for rows, s, st in zip(chunks, scores, states)]
    for rows, o in zip(chunks, outs):
        mu = jnp.mean(o, axis=-1, keepdims=True)
        oc = o - mu
        var = jnp.mean(oc * oc, axis=-1, keepdims=True)
        on = oc * lax.rsqrt(var + GN_EPS) * gn_w
        g = g_ref[rows, :].astype(F32)
        o_ref[rows, :] = (g / (1.0 + jnp.exp(-g)) * on).astype(o_ref.dtype)


def _retention(proj, gn_w, batch, seq, *, rows):
    m = proj.shape[0]
    d = HEAD_DIM
    steps = seq // rows

    def col(base):
        return pl.BlockSpec((rows, d), lambda b, h, s: (b * steps + s, base + h))

    return pl.pallas_call(
        _retention_kernel,
        grid=(batch, RET_HEADS, steps),
        in_specs=[
            col(COL_RET_Q), col(COL_RET_K), col(COL_RET_V), col(COL_RET_G),
            pl.BlockSpec((1, 4, RET_CHUNK, d), lambda b, h, s: (h, 0, 0, 0)),
            pl.BlockSpec((1, 1, d), lambda b, h, s: (h, 0, 0)),
        ],
        out_specs=pl.BlockSpec((rows, d), lambda b, h, s: (b * steps + s, h)),
        out_shape=jax.ShapeDtypeStruct((m, D_RET), BF16),
        scratch_shapes=[pltpu.VMEM((d, d), F32)],
        compiler_params=_params("parallel", "parallel", "arbitrary"),
        name="retention",
    )(proj, proj, proj, proj, _retention_tables(), gn_w.reshape(RET_HEADS, 1, d))


SB_TILE = 256


def _sb_kernel(q_ref, k_ref, v_ref, tri_ref, o_ref, acc_ref, carry_ref):
    t = SB_TILE
    to_log2_logit = HEAD_DIM ** -0.5 * math.log2(math.e)
    first_block = pl.program_id(2) * (q_ref.shape[0] // t)

    def log_gates(q, keys):
        z = _dot_nt(q, k_ref[keys, :]) * to_log2_logit
        log_beta = jnp.minimum(z, 0.0) - jnp.log2(1.0 + jnp.exp2(-jnp.abs(z)))
        return log_beta, log_beta - z

    def later_keys_sum(log_keep):
        return _dot(log_keep.astype(BF16), tri_ref[...])

    causal = lax.broadcasted_iota(jnp.int32, (t, t), 1) < lax.broadcasted_iota(jnp.int32, (t, t), 0)
    tiles = range(q_ref.shape[0] // t)
    rows = [pl.ds(i * t, t) for i in tiles]
    near = [pl.ds(pl.multiple_of((first_block + i) * t, t), t) for i in tiles]
    far = [pl.ds(pl.multiple_of(jnp.maximum(first_block + i - 1, 0) * t, t), t) for i in tiles]
    has_far = [(first_block + i > 0).astype(F32) for i in tiles]
    gates_near = [log_gates(q_ref[rows[i], :], near[i]) for i in tiles]
    gates_far = [log_gates(q_ref[rows[i], :], far[i]) for i in tiles]
    keep_near = [jnp.where(causal, gates_near[i][1], 0.0) for i in tiles]
    later_near = [later_keys_sum(keep_near[i]) for i in tiles]
    later_far = [later_keys_sum(gates_far[i][1]) for i in tiles]
    carry_near = [jnp.sum(keep_near[i], axis=-1, keepdims=True) for i in tiles]
    a_near = [jnp.where(causal, jnp.exp2(gates_near[i][0] + later_near[i]), 0.0).astype(BF16) for i in tiles]
    a_far = [jnp.exp2(gates_far[i][0] + later_far[i] + carry_near[i]).astype(BF16) for i in tiles]
    for i in tiles:
        v_far = v_ref[far[i], :] * has_far[i].astype(BF16)
        acc_ref[rows[i], :] = _dot(a_near[i], v_ref[near[i], :]) + _dot(a_far[i], v_far)
        carry_ref[rows[i], :] = carry_near[i] + jnp.sum(gates_far[i][1], axis=-1, keepdims=True) * has_far[i]

    @pl.when(jnp.max(carry_ref[...]) > SB_LOG2_ZERO)
    def _():
        def sub_tile(i, carry):
            rows = pl.ds(pl.multiple_of(i * t, t), t)
            q = q_ref[rows, :]

            def cond(state):
                key_block, max_carry = state
                return jnp.logical_and(key_block >= 0, max_carry > SB_LOG2_ZERO)

            def body(state):
                key_block, _ = state
                keys = pl.ds(pl.multiple_of(key_block * t, t), t)
                log_beta, log_keep = log_gates(q, keys)
                a = jnp.exp2(log_beta + later_keys_sum(log_keep) + carry_ref[rows, :])
                acc_ref[rows, :] += _dot(a.astype(BF16), v_ref[keys, :])
                carry_ref[rows, :] += jnp.sum(log_keep, axis=-1, keepdims=True)
                return key_block - 1, jnp.max(carry_ref[rows, :])

            lax.while_loop(cond, body, (first_block + i - 2, jnp.max(carry_ref[rows, :])))
            return carry

        lax.fori_loop(0, q_ref.shape[0] // t, sub_tile, 0)

    o_ref[...] = acc_ref[...].astype(o_ref.dtype)


def _stick_breaking(proj, batch, seq, *, rows):
    m = proj.shape[0]
    d = HEAD_DIM
    t = SB_TILE
    steps = seq // rows
    idx = np.arange(t)
    tri = jnp.asarray(idx[:, None] > idx[None, :], dtype=BF16)
    return pl.pallas_call(
        _sb_kernel,
        grid=(batch, SB_HEADS, steps),
        in_specs=[
            pl.BlockSpec((rows, d), lambda b, h, s: (b * steps + s, COL_SB_Q + h)),
            pl.BlockSpec((seq, d), lambda b, h, s: (b, COL_SB_K + h)),
            pl.BlockSpec((seq, d), lambda b, h, s: (b, COL_SB_V + h)),
            pl.BlockSpec((t, t), lambda b, h, s: (0, 0)),
        ],
        out_specs=pl.BlockSpec((rows, d), lambda b, h, s: (b * steps + s, h)),
        out_shape=jax.ShapeDtypeStruct((m, D_SB), BF16),
        scratch_shapes=[pltpu.VMEM((rows, d), F32), pltpu.VMEM((rows, 1), F32)],
        compiler_params=_params("parallel", "parallel", "arbitrary"),
        name="stick_breaking",
    )(proj, proj, proj, tri)


def _swa_bias():
    w = WINDOW
    qi = np.arange(w)[:, None]
    kj = np.arange(2 * w)[None, :]
    dist = qi + w - kj
    valid = (dist >= 0) & (dist < w)
    slopes = 2.0 ** (-(8.0 / SWA_Q_HEADS) * (np.arange(SWA_Q_HEADS, dtype=np.float64) + 1.0))
    bias = np.where(valid[None], -slopes[:, None, None] * dist[None].astype(np.float64), -np.inf)
    return jnp.asarray(bias, dtype=F32)


def _swa_kernel(sink_ref, q_ref, kc_ref, vc_ref, kp_ref, vp_ref, bias_ref, o_ref, kb_ref, vb_ref):
    w = WINDOW
    d = HEAD_DIM
    scale = d ** -0.5
    rows = q_ref.shape[0]
    kv_head = pl.program_id(1)
    has_prev = pl.program_id(2) > 0
    kb_ref[pl.ds(0, w), :] = kp_ref[...]
    kb_ref[pl.ds(w, rows), :] = kc_ref[...]
    vb_ref[pl.ds(0, w), pl.ds(0, d)] = vp_ref[...]
    vb_ref[pl.ds(w, rows), pl.ds(0, d)] = vc_ref[...]
    vb_ref[:, pl.ds(d, d)] = jnp.ones((rows + w, d), BF16)
    bias = bias_ref[0]
    sinks = [sink_ref[kv_head * SWA_GROUP + g] for g in range(SWA_GROUP)]
    blocks = range(rows // w)
    groups = range(SWA_GROUP)
    heads = [slice(g * w, (g + 1) * w) for g in groups]
    scores = []
    for blk in blocks:
        tok = pl.ds(blk * w, w)
        q = jnp.concatenate([q_ref[tok, pl.ds(g * d, d)] for g in groups], axis=0)
        s = _dot_nt(q, kb_ref[pl.ds(blk * w, 2 * w), :]) * scale + bias
        if blk == 0:
            in_current = lax.broadcasted_iota(jnp.int32, s.shape, 1) >= w
            s = jnp.where(jnp.logical_or(in_current, has_prev), s, -jnp.inf)
        scores.append(s)
    row_max = [jnp.max(s, axis=-1, keepdims=True) for s in scores]
    mx = [[jnp.maximum(row_max[blk][heads[g]], sinks[g]) for g in groups] for blk in blocks]
    probs = [jnp.concatenate([jnp.exp(scores[blk][heads[g]] - mx[blk][g]).astype(BF16) for g in groups], axis=0)
             for blk in blocks]
    outs = [_dot(probs[blk], vb_ref[pl.ds(blk * w, 2 * w), :]) for blk in blocks]
    for blk in blocks:
        for g in groups:
            denom = outs[blk][heads[g], d:] + jnp.exp(sinks[g] - mx[blk][g])
            o_ref[pl.ds(blk * w, w), pl.ds(g * d, d)] = (outs[blk][heads[g], :d] / denom).astype(o_ref.dtype)


def _sliding_window(proj, sinks, batch, seq, *, rows):
    m = proj.shape[0]
    d = HEAD_DIM
    w = WINDOW
    steps = seq // rows
    blocks_per_step = rows // w
    blocks_per_seq = seq // w

    gw = SWA_GROUP * w

    def cur(base):
        return pl.BlockSpec((rows, d), lambda b, h, s, sink: (b * steps + s, base + h))

    def prev(base):
        return pl.BlockSpec(
            (w, d), lambda b, h, s, sink: (b * blocks_per_seq + jnp.maximum(s * blocks_per_step - 1, 0), base + h))

    grid_spec = pltpu.PrefetchScalarGridSpec(
        num_scalar_prefetch=1,
        grid=(batch, SWA_KV_HEADS, steps),
        in_specs=[
            pl.BlockSpec((rows, SWA_GROUP * d), lambda b, h, s, sink: (b * steps + s, COL_SWA_Q // SWA_GROUP + h)),
            cur(COL_SWA_K), cur(COL_SWA_V), prev(COL_SWA_K), prev(COL_SWA_V),
            pl.BlockSpec((1, gw, 2 * w), lambda b, h, s, sink: (h, 0, 0)),
        ],
        out_specs=pl.BlockSpec((rows, SWA_GROUP * d), lambda b, h, s, sink: (b * steps + s, h)),
        scratch_shapes=[pltpu.VMEM((rows + w, d), BF16), pltpu.VMEM((rows + w, 2 * d), BF16)],
    )
    return pl.pallas_call(
        _swa_kernel,
        grid_spec=grid_spec,
        out_shape=jax.ShapeDtypeStruct((m, D_SWA), BF16),
        compiler_params=_params("parallel", "parallel", "arbitrary"),
        name="sliding_window",
    )(sinks.astype(F32), proj, proj, proj, proj, proj, _swa_bias().reshape(SWA_KV_HEADS, gw, 2 * w))


OUT_PROJ_ROWS = 128


def _out_proj_kernel(ret_ref, sb_ref, swa_ref, w_ref, gain_ref, next_gain_ref, x_ref, o_ref, h_ref):
    for r in range(x_ref.shape[0] // OUT_PROJ_ROWS):
        rows = pl.ds(r * OUT_PROJ_ROWS, OUT_PROJ_ROWS)
        y = _dot(ret_ref[rows, :], w_ref[pl.ds(0, D_RET), :])
        y += _dot(sb_ref[rows, :], w_ref[pl.ds(D_RET, D_SB), :])
        y += _dot(swa_ref[rows, :], w_ref[pl.ds(D_RET + D_SB, D_SWA), :])
        x = x_ref[rows, :] + _rms_scale(y, gain_ref[...])
        o_ref[rows, :] = x
        h_ref[rows, :] = _rms_scale(x, next_gain_ref[...]).astype(h_ref.dtype)


def _out_proj(o_ret, o_sb, o_swa, w, layer, gain, next_gain, x, *, tm):
    m, d = x.shape

    def rows(width):
        return pl.BlockSpec((tm, width), lambda i: (i, 0))

    vec = pl.BlockSpec((1, d), lambda i: (0, 0))
    return pl.pallas_call(
        _out_proj_kernel,
        grid=(m // tm,),
        in_specs=[
            rows(D_RET), rows(D_SB), rows(D_SWA),
            pl.BlockSpec((None,) + w.shape[1:], lambda i: (layer, 0, 0)),
            vec, vec, rows(d),
        ],
        out_specs=[rows(d), rows(d)],
        out_shape=[jax.ShapeDtypeStruct((m, d), F32), jax.ShapeDtypeStruct((m, d), BF16)],
        compiler_params=_params("parallel"),
        name="out_proj_norm_res",
    )(o_ret, o_sb, o_swa, w, gain.reshape(1, d), next_gain.reshape(1, d), x)


TAIL = 8
assert TAIL >= CONV_WIDTH - 1
FFN_UP_HALVES = 2
FFN_UP_CHUNKS = (1536, 1280)
FFN_UP_ROWS = 512


def _gelu_tanh(x):
    return 0.5 * x * (1.0 + jnp.tanh(math.sqrt(2.0 / math.pi) * (x + 0.044715 * (x * x * x))))


def _ffn_up_kernel(h_ref, wa_ref, wb_ref, cw_ref, cb_ref, o_ref, a_ref, b_ref, tail_ref, *, tiles_per_seq):
    tm = h_ref.shape[0]
    starts_sequence = pl.program_id(1) % tiles_per_seq == 0

    @pl.when(starts_sequence)
    def _():
        a_ref[pl.ds(0, TAIL), :] = jnp.zeros((TAIL, a_ref.shape[1]), F32)

    @pl.when(jnp.logical_not(starts_sequence))
    def _():
        a_ref[pl.ds(0, TAIL), :] = tail_ref[...]

    tn = a_ref.shape[1]
    assert sum(FFN_UP_CHUNKS) == tn
    chunks = [pl.ds(sum(FFN_UP_CHUNKS[:c]), width) for c, width in enumerate(FFN_UP_CHUNKS)]

    def matmuls(cols):
        a_ref[pl.ds(TAIL, tm), cols] = _dot(h_ref[...], wa_ref[:, cols])
        b_ref[:, cols] = _dot(h_ref[...], wb_ref[:, cols])

    def conv_gate(cols):
        for r in range(0, tm, FFN_UP_ROWS):
            a_conv = cb_ref[:, cols]
            for tap in range(CONV_WIDTH):
                taps = pl.ds(r + TAIL - (CONV_WIDTH - 1) + tap, FFN_UP_ROWS)
                a_conv = a_conv + a_ref[taps, cols] * cw_ref[pl.ds(tap, 1), cols]
            rows = pl.ds(r, FFN_UP_ROWS)
            o_ref[rows, cols] = (_gelu_tanh(a_conv) * b_ref[rows, cols]).astype(o_ref.dtype)

    matmuls(chunks[0])
    for c, cols in enumerate(chunks):
        if c + 1 < len(chunks):
            matmuls(chunks[c + 1])
        conv_gate(cols)
    tail_ref[...] = a_ref[pl.ds(tm, TAIL), :]


def _ffn_up(h, w_up, layer, conv_w, conv_b, seq, *, tm):
    m, d = h.shape
    f = conv_w.shape[1]
    tn = f // FFN_UP_HALVES
    resident = dict(pipeline_mode=pl.Buffered(1))
    return pl.pallas_call(
        functools.partial(_ffn_up_kernel, tiles_per_seq=seq // tm),
        grid=(FFN_UP_HALVES, m // tm),
        in_specs=[
            pl.BlockSpec((tm, d), lambda j, i: (i, 0)),
            pl.BlockSpec((None, d, tn), lambda j, i: (layer, 0, j), **resident),
            pl.BlockSpec((None, d, tn), lambda j, i: (layer, 0, FFN_UP_HALVES + j), **resident),
            pl.BlockSpec((CONV_WIDTH, tn), lambda j, i: (0, j)),
            pl.BlockSpec((1, tn), lambda j, i: (0, j)),
        ],
        out_specs=pl.BlockSpec((tm, tn), lambda j, i: (i, j)),
        out_shape=jax.ShapeDtypeStruct((m, f), BF16),
        scratch_shapes=[pltpu.VMEM((TAIL + tm, tn), F32), pltpu.VMEM((tm, tn), F32), pltpu.VMEM((TAIL, tn), F32)],
        compiler_params=_params("arbitrary", "arbitrary"),
        name="ffn_up_conv_gate",
    )(h, w_up, w_up, conv_w, conv_b.reshape(1, f))


def _ffn_down_kernel(g_ref, w_ref, gain_ref, next_gain_ref, x_ref, o_ref, *maybe_h_ref):
    for r in range(x_ref.shape[0] // OUT_PROJ_ROWS):
        rows = pl.ds(r * OUT_PROJ_ROWS, OUT_PROJ_ROWS)
        x = x_ref[rows, :] + _rms_scale(_dot(g_ref[rows, :], w_ref[...]), gain_ref[...])
        o_ref[rows, :] = x
        for h_ref in maybe_h_ref:
            h_ref[rows, :] = _rms_scale(x, next_gain_ref[...]).astype(h_ref.dtype)


def _ffn_down(g, w, layer, gain, next_gain, x, *, tm):
    m, d = x.shape
    f = g.shape[1]
    rows = pl.BlockSpec((tm, d), lambda i: (i, 0))
    vec = pl.BlockSpec((1, d), lambda i: (0, 0))
    emit_next = next_gain is not None
    out = pl.pallas_call(
        _ffn_down_kernel,
        grid=(m // tm,),
        in_specs=[
            pl.BlockSpec((tm, f), lambda i: (i, 0)),
            pl.BlockSpec((None, f, d), lambda i: (layer, 0, 0), pipeline_mode=pl.Buffered(1)),
            vec, vec, rows,
        ],
        out_specs=[rows, rows] if emit_next else [rows],
        out_shape=[jax.ShapeDtypeStruct((m, d), F32)] + ([jax.ShapeDtypeStruct((m, d), BF16)] if emit_next else []),
        compiler_params=_params("parallel"),
        name="ffn_down_norm_res",
    )(g, w, gain.reshape(1, d), (next_gain if emit_next else gain).reshape(1, d), x)
    return (out[0], out[1]) if emit_next else (out[0], None)


def _tile(extent, preferred):
    t = min(extent, preferred)
    while extent % t:
        t //= 2
    return t


def kernel(x, w_in, w_out, ret_gn_w, swa_sinks, norm_mix_pre, norm_mix_post, norm_ffn_pre, norm_ffn_post, w_up, conv_w, conv_b, w_down):
    batch, seq, d_model = x.shape
    depth = w_in.shape[0]
    d_ff = conv_w.shape[-1]
    assert seq % SB_TILE == 0 and d_ff % 512 == 0
    xf = x.reshape(batch * seq, d_model).astype(F32)
    w_in, w_out, w_up, w_down = (w.astype(BF16) for w in (w_in, w_out, w_up, w_down))
    tm_big = _tile(seq, 1024)
    tm_small = _tile(seq, 512)
    h = None
    for l in range(depth):
        if h is None:
            proj = _in_proj(xf, norm_mix_pre[l], w_in, l, tm=tm_small)
        else:
            proj = _in_proj(h, None, w_in, l, tm=tm_small)
        o_ret = _retention(proj, ret_gn_w[l], batch, seq, rows=_tile(seq, 4096))
        o_sb = _stick_breaking(proj, batch, seq, rows=_tile(seq, 2048))
        o_swa = _sliding_window(proj, swa_sinks[l], batch, seq, rows=_tile(seq, 2048))
        xf, h = _out_proj(o_ret, o_sb, o_swa, w_out, l, norm_mix_post[l], norm_ffn_pre[l], xf, tm=tm_small)
        g = _ffn_up(h, w_up, l, conv_w[l], conv_b[l], seq, tm=tm_small)
        next_gain = norm_mix_pre[l + 1] if l + 1 < depth else None
        xf, h = _ffn_down(g, w_down, l, norm_ffn_post[l], next_gain, xf, tm=tm_small)
    return xf.reshape(batch, seq, d_model).astype(x.dtype)
```

```python
import functools
import math

import jax
import jax.numpy as jnp
import numpy as np
from jax import lax
from jax.experimental import pallas as pl
from jax.experimental.pallas import tpu as pltpu

HEAD_DIM = 128
RET_HEADS = 4
SB_HEADS = 4
SWA_Q_HEADS = 8
SWA_KV_HEADS = 2
SWA_GROUP = SWA_Q_HEADS // SWA_KV_HEADS
RET_CHUNK = 128
WINDOW = 128
CONV_WIDTH = 3
RMS_EPS = 1e-6
GN_EPS = 1e-5

COL_RET_Q, COL_RET_K, COL_RET_V, COL_RET_G = 0, 4, 8, 12
COL_SB_Q, COL_SB_K, COL_SB_V = 16, 20, 24
COL_SWA_Q, COL_SWA_K, COL_SWA_V = 28, 36, 38
D_RET = RET_HEADS * HEAD_DIM
D_SB = SB_HEADS * HEAD_DIM
D_SWA = SWA_Q_HEADS * HEAD_DIM

LANES = 128
VMEM_LIMIT_BYTES = 56 * 1024 * 1024

SB_LOG2_ZERO = -151.0

F32 = jnp.float32
BF16 = jnp.bfloat16


def _params(*semantics):
    return pltpu.CompilerParams(dimension_semantics=semantics, vmem_limit_bytes=VMEM_LIMIT_BYTES)


def _rms_scale(x, gain):
    ms = jnp.mean(x * x, axis=-1, keepdims=True)
    return x * lax.rsqrt(ms + RMS_EPS) * gain


def _dot(a, b):
    return jnp.dot(a, b, preferred_element_type=F32)


def _dot_nt(a, b):
    return lax.dot_general(a, b, (((1,), (1,)), ((), ())), preferred_element_type=F32)


def _dot_tn(a, b):
    return lax.dot_general(a, b, (((0,), (0,)), ((), ())), preferred_element_type=F32)


IN_PROJ_NORM_ROWS = 128
CAST_ROWS = 32


def _norm_matmul_kernel(x_ref, gain_ref, w_ref, *refs):
    n_casts = (len(refs) - 1) // 2
    cast_in, o_ref, cast_out = refs[:n_casts], refs[n_casts], refs[n_casts + 1:]
    for src_ref, dst_ref in zip(cast_in, cast_out):
        dst_ref[...] = src_ref[...].astype(dst_ref.dtype)
    for r in range(0, x_ref.shape[0], IN_PROJ_NORM_ROWS):
        rows = pl.ds(r, IN_PROJ_NORM_ROWS)
        h = _rms_scale(x_ref[rows, :], gain_ref[...]).astype(BF16)
        o_ref[rows, :] = _dot(h, w_ref[...]).astype(o_ref.dtype)


def _matmul_kernel(h_ref, w_ref, o_ref):
    o_ref[...] = _dot(h_ref[...], w_ref[...]).astype(o_ref.dtype)


def _in_proj(x, gain, w, layer, *, tm, cast_along=()):
    m, d = x.shape
    n = w.shape[2]
    steps = m // tm
    rows = pl.BlockSpec((tm, d), lambda i: (i, 0))
    w_spec = pl.BlockSpec((None, d, n), lambda i: (layer, 0, 0), pipeline_mode=pl.Buffered(1))
    out_spec = pl.BlockSpec((tm, n), lambda i: (i, 0))
    out_shape = jax.ShapeDtypeStruct((m, n), BF16)
    if gain is None:
        assert not cast_along
        return pl.pallas_call(
            _matmul_kernel, grid=(steps,), in_specs=[rows, w_spec], out_specs=out_spec, out_shape=out_shape,
            compiler_params=_params("parallel"), name="in_proj")(x, w)
    assert all(a.size % (steps * CAST_ROWS * LANES) == 0 for a in cast_along)
    views = [a.reshape(steps * CAST_ROWS, a.size // (steps * CAST_ROWS)) for a in cast_along]
    slabs = [pl.BlockSpec((CAST_ROWS, v.shape[1]), lambda i: (i, 0)) for v in views]
    out = pl.pallas_call(
        _norm_matmul_kernel,
        grid=(steps,),
        in_specs=[rows, pl.BlockSpec((1, d), lambda i: (0, 0)), w_spec] + slabs,
        out_specs=[out_spec] + slabs,
        out_shape=[out_shape] + [jax.ShapeDtypeStruct(v.shape, BF16) for v in views],
        compiler_params=_params("parallel"),
        name="norm_in_proj",
    )(x, gain.reshape(1, d), w, *views)
    return out[0], [o.reshape(a.shape) for o, a in zip(out[1:], cast_along)]


def _retention_tables():
    c = RET_CHUNK
    h = np.arange(RET_HEADS, dtype=np.float64)
    log_gamma = np.log(1.0 - 2.0 ** (-5.0 - h))
    idx = np.arange(c, dtype=np.float64)
    diff = idx[:, None] - idx[None, :]
    scale = HEAD_DIM ** -0.5
    decay = np.where(diff >= 0, np.exp(log_gamma[:, None, None] * np.maximum(diff, 0.0)), 0.0) * scale
    xi = np.exp(log_gamma[:, None] * (idx + 1.0)[None, :])
    zeta = np.exp(log_gamma[:, None] * (c - 1.0 - idx)[None, :]) * scale
    chunk_decay = np.exp(log_gamma * c)
    ones = np.ones((RET_HEADS, c, HEAD_DIM))
    t = np.stack([decay, xi[:, :, None] * ones, zeta[:, :, None] * ones, chunk_decay[:, None, None] * ones], axis=1)
    return jnp.asarray(t, dtype=F32)


def _retention_kernel(q_ref, k_ref, v_ref, g_ref, tab_ref, gn_ref, o_ref, state_ref):
    @pl.when(pl.program_id(2) == 0)
    def _():
        state_ref[...] = jnp.zeros_like(state_ref)

    c = RET_CHUNK
    decay = tab_ref[0, 0]
    xi = tab_ref[0, 1]
    zeta = tab_ref[0, 2]
    chunk_decay = tab_ref[0, 3]
    gn_w = gn_ref[0]

    chunks = [pl.ds(n * c, c) for n in range(q_ref.shape[0] // c)]
    scores = [(_dot_nt(q_ref[rows, :], k_ref[rows, :]) * decay).astype(BF16) for rows in chunks]
    contribs = [_dot_tn((k_ref[rows, :].astype(F32) * zeta).astype(BF16), v_ref[rows, :]) for rows in chunks]
    states = []
    state = state_ref[...]
    for contrib in contribs:
        states.append(state.astype(BF16))
        state = state * chunk_decay + contrib
    state_ref[...] = state
    outs = [_dot(s, v_ref[rows, :]) + _dot(q_ref[rows, :], st) * xi for rows, s, st in zip(chunks, scores, states)]
    for rows, o in zip(chunks, outs):
        mu = jnp.mean(o, axis=-1, keepdims=True)
        oc = o - mu
        var = jnp.mean(oc * oc, axis=-1, keepdims=True)
        on = oc * lax.rsqrt(var + GN_EPS) * gn_w
        g = g_ref[rows, :].astype(F32)
        o_ref[rows, :] = (g / (1.0 + jnp.exp(-g)) * on).astype(o_ref.dtype)


def _retention(proj, gn_w, batch, seq, *, rows):
    m = proj.shape[0]
    d = HEAD_DIM
    steps = seq // rows

    def col(base):
        return pl.BlockSpec((rows, d), lambda b, h, s: (b * steps + s, base + h))

    return pl.pallas_call(
        _retention_kernel,
        grid=(batch, RET_HEADS, steps),
        in_specs=[
            col(COL_RET_Q), col(COL_RET_K), col(COL_RET_V), col(COL_RET_G),
            pl.BlockSpec((1, 4, RET_CHUNK, d), lambda b, h, s: (h, 0, 0, 0)),
            pl.BlockSpec((1, 1, d), lambda b, h, s: (h, 0, 0)),
        ],
        out_specs=pl.BlockSpec((rows, d), lambda b, h, s: (b * steps + s, h)),
        out_shape=jax.ShapeDtypeStruct((m, D_RET), BF16),
        scratch_shapes=[pltpu.VMEM((d, d), F32)],
        compiler_params=_params("parallel", "parallel", "arbitrary"),
        name="retention",
    )(proj, proj, proj, proj, _retention_tables(), gn_w.reshape(RET_HEADS, 1, d))


SB_TILE = 256


def _sb_kernel(q_ref, k_ref, v_ref, tri_ref, o_ref, acc_ref, carry_ref):
    t = SB_TILE
    to_log2_logit = HEAD_DIM ** -0.5 * math.log2(math.e)
    first_block = pl.program_id(2) * (q_ref.shape[0] // t)

    def log_gates(q, keys):
        z = _dot_nt(q, k_ref[keys, :]) * to_log2_logit
        log_beta = jnp.minimum(z, 0.0) - jnp.log2(1.0 + jnp.exp2(-jnp.abs(z)))
        return log_beta, log_beta - z

    def later_keys_sum(log_keep):
        return _dot(log_keep.astype(BF16), tri_ref[...])

    causal = lax.broadcasted_iota(jnp.int32, (t, t), 1) < lax.broadcasted_iota(jnp.int32, (t, t), 0)
    tiles = range(q_ref.shape[0] // t)
    rows = [pl.ds(i * t, t) for i in tiles]
    near = [pl.ds(pl.multiple_of((first_block + i) * t, t), t) for i in tiles]
    far = [pl.ds(pl.multiple_of(jnp.maximum(first_block + i - 1, 0) * t, t), t) for i in tiles]
    has_far = [(first_block + i > 0).astype(F32) for i in tiles]
    gates_near = [log_gates(q_ref[rows[i], :], near[i]) for i in tiles]
    gates_far = [log_gates(q_ref[rows[i], :], far[i]) for i in tiles]
    keep_near = [jnp.where(causal, gates_near[i][1], 0.0) for i in tiles]
    later_near = [later_keys_sum(keep_near[i]) for i in tiles]
    later_far = [later_keys_sum(gates_far[i][1]) for i in tiles]
    carry_near = [jnp.sum(keep_near[i], axis=-1, keepdims=True) for i in tiles]
    a_near = [jnp.where(causal, jnp.exp2(gates_near[i][0] + later_near[i]), 0.0).astype(BF16) for i in tiles]
    a_far = [jnp.exp2(gates_far[i][0] + later_far[i] + carry_near[i]).astype(BF16) for i in tiles]
    for i in tiles:
        v_far = v_ref[far[i], :] * has_far[i].astype(BF16)
        acc_ref[rows[i], :] = _dot(a_near[i], v_ref[near[i], :]) + _dot(a_far[i], v_far)
        carry_ref[rows[i], :] = carry_near[i] + jnp.sum(gates_far[i][1], axis=-1, keepdims=True) * has_far[i]

    @pl.when(jnp.max(carry_ref[...]) > SB_LOG2_ZERO)
    def _():
        def sub_tile(i, carry):
            rows = pl.ds(pl.multiple_of(i * t, t), t)
            q = q_ref[rows, :]

            def cond(state):
                key_block, max_carry = state
                return jnp.logical_and(key_block >= 0, max_carry > SB_LOG2_ZERO)

            def body(state):
                key_block, _ = state
                keys = pl.ds(pl.multiple_of(key_block * t, t), t)
                log_beta, log_keep = log_gates(q, keys)
                a = jnp.exp2(log_beta + later_keys_sum(log_keep) + carry_ref[rows, :])
                acc_ref[rows, :] += _dot(a.astype(BF16), v_ref[keys, :])
                carry_ref[rows, :] += jnp.sum(log_keep, axis=-1, keepdims=True)
                return key_block - 1, jnp.max(carry_ref[rows, :])

            lax.while_loop(cond, body, (first_block + i - 2, jnp.max(carry_ref[rows, :])))
            return carry

        lax.fori_loop(0, q_ref.shape[0] // t, sub_tile, 0)

    o_ref[...] = acc_ref[...].astype(o_ref.dtype)


def _stick_breaking(proj, batch, seq, *, rows):
    m = proj.shape[0]
    d = HEAD_DIM
    t = SB_TILE
    steps = seq // rows
    idx = np.arange(t)
    tri = jnp.asarray(idx[:, None] > idx[None, :], dtype=BF16)
    return pl.pallas_call(
        _sb_kernel,
        grid=(batch, SB_HEADS, steps),
        in_specs=[
            pl.BlockSpec((rows, d), lambda b, h, s: (b * steps + s, COL_SB_Q + h)),
            pl.BlockSpec((seq, d), lambda b, h, s: (b, COL_SB_K + h)),
            pl.BlockSpec((seq, d), lambda b, h, s: (b, COL_SB_V + h)),
            pl.BlockSpec((t, t), lambda b, h, s: (0, 0)),
        ],
        out_specs=pl.BlockSpec((rows, d), lambda b, h, s: (b * steps + s, h)),
        out_shape=jax.ShapeDtypeStruct((m, D_SB), BF16),
        scratch_shapes=[pltpu.VMEM((rows, d), F32), pltpu.VMEM((rows, 1), F32)],
        compiler_params=_params("parallel", "parallel", "arbitrary"),
        name="stick_breaking",
    )(proj, proj, proj, tri)


def _swa_bias():
    w = WINDOW
    qi = np.arange(w)[:, None]
    kj = np.arange(2 * w)[None, :]
    dist = qi + w - kj
    valid = (dist >= 0) & (dist < w)
    slopes = 2.0 ** (-(8.0 / SWA_Q_HEADS) * (np.arange(SWA_Q_HEADS, dtype=np.float64) + 1.0))
    bias = np.where(valid[None], -slopes[:, None, None] * dist[None].astype(np.float64), -np.inf)
    return jnp.asarray(bias, dtype=F32)


def _swa_kernel(sink_ref, q_ref, kc_ref, vc_ref, kp_ref, vp_ref, bias_ref, o_ref, kb_ref, vb_ref):
    w = WINDOW
    d = HEAD_DIM
    scale = d ** -0.5
    rows = q_ref.shape[0]
    kv_head = pl.program_id(1)
    has_prev = pl.program_id(2) > 0
    kb_ref[pl.ds(0, w), :] = kp_ref[...]
    kb_ref[pl.ds(w, rows), :] = kc_ref[...]
    vb_ref[pl.ds(0, w), pl.ds(0, d)] = vp_ref[...]
    vb_ref[pl.ds(w, rows), pl.ds(0, d)] = vc_ref[...]
    vb_ref[:, pl.ds(d, d)] = jnp.ones((rows + w, d), BF16)
    bias = bias_ref[0]
    sinks = [sink_ref[kv_head * SWA_GROUP + g] for g in range(SWA_GROUP)]
    blocks = range(rows // w)
    groups = range(SWA_GROUP)
    heads = [slice(g * w, (g + 1) * w) for g in groups]
    scores = []
    for blk in blocks:
        tok = pl.ds(blk * w, w)
        q = jnp.concatenate([q_ref[tok, pl.ds(g * d, d)] for g in groups], axis=0)
        s = _dot_nt(q, kb_ref[pl.ds(blk * w, 2 * w), :]) * scale + bias
        if blk == 0:
            in_current = lax.broadcasted_iota(jnp.int32, s.shape, 1) >= w
            s = jnp.where(jnp.logical_or(in_current, has_prev), s, -jnp.inf)
        scores.append(s)
    row_max = [jnp.max(s, axis=-1, keepdims=True) for s in scores]
    mx = [[jnp.maximum(row_max[blk][heads[g]], sinks[g]) for g in groups] for blk in blocks]
    probs = [jnp.concatenate([jnp.exp(scores[blk][heads[g]] - mx[blk][g]).astype(BF16) for g in groups], axis=0)
             for blk in blocks]
    outs = [_dot(probs[blk], vb_ref[pl.ds(blk * w, 2 * w), :]) for blk in blocks]
    for blk in blocks:
        for g in groups:
            denom = outs[blk][heads[g], d:] + jnp.exp(sinks[g] - mx[blk][g])
            o_ref[pl.ds(blk * w, w), pl.ds(g * d, d)] = (outs[blk][heads[g], :d] / denom).astype(o_ref.dtype)


def _sliding_window(proj, sinks, batch, seq, *, rows):
    m = proj.shape[0]
    d = HEAD_DIM
    w = WINDOW
    steps = seq // rows
    blocks_per_step = rows // w
    blocks_per_seq = seq // w

    gw = SWA_GROUP * w

    def cur(base):
        return pl.BlockSpec((rows, d), lambda b, h, s, sink: (b * steps + s, base + h))

    def prev(base):
        return pl.BlockSpec(
            (w, d), lambda b, h, s, sink: (b * blocks_per_seq + jnp.maximum(s * blocks_per_step - 1, 0), base + h))

    grid_spec = pltpu.PrefetchScalarGridSpec(
        num_scalar_prefetch=1,
        grid=(batch, SWA_KV_HEADS, steps),
        in_specs=[
            pl.BlockSpec((rows, SWA_GROUP * d), lambda b, h, s, sink: (b * steps + s, COL_SWA_Q // SWA_GROUP + h)),
            cur(COL_SWA_K), cur(COL_SWA_V), prev(COL_SWA_K), prev(COL_SWA_V),
            pl.BlockSpec((1, gw, 2 * w), lambda b, h, s, sink: (h, 0, 0)),
        ],
        out_specs=pl.BlockSpec((rows, SWA_GROUP * d), lambda b, h, s, sink: (b * steps + s, h)),
        scratch_shapes=[pltpu.VMEM((rows + w, d), BF16), pltpu.VMEM((rows + w, 2 * d), BF16)],
    )
    return pl.pallas_call(
        _swa_kernel,
        grid_spec=grid_spec,
        out_shape=jax.ShapeDtypeStruct((m, D_SWA), BF16),
        compiler_params=_params("parallel", "parallel", "arbitrary"),
        name="sliding_window",
    )(sinks.astype(F32), proj, proj, proj, proj, proj, _swa_bias().reshape(SWA_KV_HEADS, gw, 2 * w))


EPILOGUE_ROWS = 128


def _row_chunks(rows):
    size = min(rows, EPILOGUE_ROWS)
    return [pl.ds(r, size) for r in range(0, rows, size)]


def _out_proj_kernel(ret_ref, sb_ref, swa_ref, w_ref, gain_ref, next_gain_ref, x_ref, o_ref, h_ref):
    for rows in _row_chunks(x_ref.shape[0]):
        y = _dot(ret_ref[rows, :], w_ref[pl.ds(0, D_RET), :])
        y += _dot(sb_ref[rows, :], w_ref[pl.ds(D_RET, D_SB), :])
        y += _dot(swa_ref[rows, :], w_ref[pl.ds(D_RET + D_SB, D_SWA), :])
        x = x_ref[rows, :] + _rms_scale(y, gain_ref[...])
        o_ref[rows, :] = x
        h_ref[rows, :] = _rms_scale(x, next_gain_ref[...]).astype(h_ref.dtype)


def _out_proj(o_ret, o_sb, o_swa, w, layer, gain, next_gain, x, *, tm):
    m, d = x.shape

    def rows(width):
        return pl.BlockSpec((tm, width), lambda i: (i, 0))

    vec = pl.BlockSpec((1, d), lambda i: (0, 0))
    return pl.pallas_call(
        _out_proj_kernel,
        grid=(m // tm,),
        in_specs=[
            rows(D_RET), rows(D_SB), rows(D_SWA),
            pl.BlockSpec((None,) + w.shape[1:], lambda i: (layer, 0, 0)),
            vec, vec, rows(d),
        ],
        out_specs=[rows(d), rows(d)],
        out_shape=[jax.ShapeDtypeStruct((m, d), F32), jax.ShapeDtypeStruct((m, d), BF16)],
        compiler_params=_params("parallel"),
        name="out_proj_norm_res",
    )(o_ret, o_sb, o_swa, w, gain.reshape(1, d), next_gain.reshape(1, d), x)


TAIL = 8
assert TAIL >= CONV_WIDTH - 1
FFN_UP_HALVES = 2
FFN_UP_CHUNKS = (1792, 1024)
FFN_UP_ROWS = 512


def _gelu_tanh(x):
    return 0.5 * x * (1.0 + jnp.tanh(math.sqrt(2.0 / math.pi) * (x + 0.044715 * (x * x * x))))


def _ffn_up_kernel(h_ref, wa_ref, wb_ref, cw_ref, cb_ref, o_ref, a_ref, b_ref, tail_ref, *, tiles_per_seq):
    tm = h_ref.shape[0]
    starts_sequence = pl.program_id(1) % tiles_per_seq == 0

    @pl.when(starts_sequence)
    def _():
        a_ref[pl.ds(0, TAIL), :] = jnp.zeros((TAIL, a_ref.shape[1]), F32)

    @pl.when(jnp.logical_not(starts_sequence))
    def _():
        a_ref[pl.ds(0, TAIL), :] = tail_ref[...]

    tn = a_ref.shape[1]
    assert sum(FFN_UP_CHUNKS) == tn
    chunks = [pl.ds(sum(FFN_UP_CHUNKS[:c]), width) for c, width in enumerate(FFN_UP_CHUNKS)]

    def matmuls(cols):
        a_ref[pl.ds(TAIL, tm), cols] = _dot(h_ref[...], wa_ref[:, cols])
        b_ref[:, cols] = _dot(h_ref[...], wb_ref[:, cols])

    def conv_gate(cols):
        for r in range(0, tm, FFN_UP_ROWS):
            a_conv = cb_ref[:, cols]
            for tap in range(CONV_WIDTH):
                taps = pl.ds(r + TAIL - (CONV_WIDTH - 1) + tap, FFN_UP_ROWS)
                a_conv = a_conv + a_ref[taps, cols] * cw_ref[pl.ds(tap, 1), cols]
            rows = pl.ds(r, FFN_UP_ROWS)
            o_ref[rows, cols] = (_gelu_tanh(a_conv) * b_ref[rows, cols]).astype(o_ref.dtype)

    matmuls(chunks[0])
    for c, cols in enumerate(chunks):
        if c + 1 < len(chunks):
            matmuls(chunks[c + 1])
        conv_gate(cols)
    tail_ref[...] = a_ref[pl.ds(tm, TAIL), :]


def _ffn_up(h, w_up, layer, conv_w, conv_b, seq, *, tm):
    m, d = h.shape
    f = conv_w.shape[1]
    tn = f // FFN_UP_HALVES
    resident = dict(pipeline_mode=pl.Buffered(1))
    return pl.pallas_call(
        functools.partial(_ffn_up_kernel, tiles_per_seq=seq // tm),
        grid=(FFN_UP_HALVES, m // tm),
        in_specs=[
            pl.BlockSpec((tm, d), lambda j, i: (i, 0)),
            pl.BlockSpec((None, d, tn), lambda j, i: (layer, 0, j), **resident),
            pl.BlockSpec((None, d, tn), lambda j, i: (layer, 0, FFN_UP_HALVES + j), **resident),
            pl.BlockSpec((CONV_WIDTH, tn), lambda j, i: (0, j)),
            pl.BlockSpec((1, tn), lambda j, i: (0, j)),
        ],
        out_specs=pl.BlockSpec((tm, tn), lambda j, i: (i, j)),
        out_shape=jax.ShapeDtypeStruct((m, f), BF16),
        scratch_shapes=[pltpu.VMEM((TAIL + tm, tn), F32), pltpu.VMEM((tm, tn), F32), pltpu.VMEM((TAIL, tn), F32)],
        compiler_params=_params("arbitrary", "arbitrary"),
        name="ffn_up_conv_gate",
    )(h, w_up, w_up, conv_w, conv_b.reshape(1, f))


def _ffn_down_kernel(g_ref, w_ref, gain_ref, next_gain_ref, x_ref, o_ref, *maybe_h_ref):
    for rows in _row_chunks(x_ref.shape[0]):
        x = x_ref[rows, :] + _rms_scale(_dot(g_ref[rows, :], w_ref[...]), gain_ref[...])
        o_ref[rows, :] = x
        for h_ref in maybe_h_ref:
            h_ref[rows, :] = _rms_scale(x, next_gain_ref[...]).astype(h_ref.dtype)


def _ffn_down(g, w, layer, gain, next_gain, x, *, tm):
    m, d = x.shape
    f = g.shape[1]
    rows = pl.BlockSpec((tm, d), lambda i: (i, 0))
    vec = pl.BlockSpec((1, d), lambda i: (0, 0))
    emit_next = next_gain is not None
    out = pl.pallas_call(
        _ffn_down_kernel,
        grid=(m // tm,),
        in_specs=[
            pl.BlockSpec((tm, f), lambda i: (i, 0)),
            pl.BlockSpec((None, f, d), lambda i: (layer, 0, 0), pipeline_mode=pl.Buffered(1)),
            vec, vec, rows,
        ],
        out_specs=[rows, rows] if emit_next else [rows],
        out_shape=[jax.ShapeDtypeStruct((m, d), F32)] + ([jax.ShapeDtypeStruct((m, d), BF16)] if emit_next else []),
        compiler_params=_params("parallel"),
        name="ffn_down_norm_res",
    )(g, w, gain.reshape(1, d), (next_gain if emit_next else gain).reshape(1, d), x)
    return (out[0], out[1]) if emit_next else (out[0], None)


def _tile(extent, preferred):
    t = min(extent, preferred)
    while extent % t:
        t //= 2
    return t


def kernel(x, w_in, w_out, ret_gn_w, swa_sinks, norm_mix_pre, norm_mix_post, norm_ffn_pre, norm_ffn_post, w_up, conv_w, conv_b, w_down):
    batch, seq, d_model = x.shape
    depth = w_in.shape[0]
    d_ff = conv_w.shape[-1]
    assert seq % SB_TILE == 0 and d_ff % 512 == 0
    xf = x.reshape(batch * seq, d_model).astype(F32)
    w_in, w_out = w_in.astype(BF16), w_out.astype(BF16)
    tm_small = _tile(seq, 512)
    h = None
    for l in range(depth):
        if h is None:
            proj, (w_up, w_down) = _in_proj(xf, norm_mix_pre[l], w_in, l, tm=tm_small, cast_along=(w_up, w_down))
        else:
            proj = _in_proj(h, None, w_in, l, tm=tm_small)
        o_ret = _retention(proj, ret_gn_w[l], batch, seq, rows=_tile(seq, 4096))
        o_sb = _stick_breaking(proj, batch, seq, rows=_tile(seq, 2048))
        o_swa = _sliding_window(proj, swa_sinks[l], batch, seq, rows=_tile(seq, 2048))
        xf, h = _out_proj(o_ret, o_sb, o_swa, w_out, l, norm_mix_post[l], norm_ffn_pre[l], xf, tm=tm_small)
        g = _ffn_up(h, w_up, l, conv_w[l], conv_b[l], seq, tm=tm_small)
        next_gain = norm_mix_pre[l + 1] if l + 1 < depth else None
        xf, h = _ffn_down(g, w_down, l, norm_ffn_post[l], next_gain, xf, tm=tm_small)
    return xf.reshape(batch, seq, d_model).astype(x.dtype)
```

```python
import functools
import math

import jax
import jax.numpy as jnp
import numpy as np
from jax import lax
from jax.experimental import pallas as pl
from jax.experimental.pallas import tpu as pltpu

HEAD_DIM = 128
RET_HEADS = 4
SB_HEADS = 4
SWA_Q_HEADS = 8
SWA_KV_HEADS = 2
SWA_GROUP = SWA_Q_HEADS // SWA_KV_HEADS
RET_CHUNK = 128
WINDOW = 128
CONV_WIDTH = 3
RMS_EPS = 1e-6
GN_EPS = 1e-5

COL_RET_Q, COL_RET_K, COL_RET_V, COL_RET_G = 0, 4, 8, 12
COL_SB_Q, COL_SB_K, COL_SB_V = 16, 20, 24
COL_SWA_Q, COL_SWA_K, COL_SWA_V = 28, 36, 38
D_RET = RET_HEADS * HEAD_DIM
D_SB = SB_HEADS * HEAD_DIM
D_SWA = SWA_Q_HEADS * HEAD_DIM

VMEM_LIMIT_BYTES = 56 * 1024 * 1024

SB_LOG2_ZERO = -151.0

F32 = jnp.float32
BF16 = jnp.bfloat16


def _params(*semantics):
    return pltpu.CompilerParams(dimension_semantics=semantics, vmem_limit_bytes=VMEM_LIMIT_BYTES)


def _rms_scale(x, gain):
    ms = jnp.mean(x * x, axis=-1, keepdims=True)
    return x * lax.rsqrt(ms + RMS_EPS) * gain


def _dot(a, b):
    return jnp.dot(a, b, preferred_element_type=F32)


def _dot_nt(a, b):
    return lax.dot_general(a, b, (((1,), (1,)), ((), ())), preferred_element_type=F32)


def _dot_tn(a, b):
    return lax.dot_general(a, b, (((0,), (0,)), ((), ())), preferred_element_type=F32)


IN_PROJ_NORM_ROWS = 128
BF16_SUBLANES = 16


def _norm_matmul_kernel(x_ref, gain_ref, w_ref, *refs):
    n_casts = (len(refs) - 1) // 2
    cast_in, o_ref, cast_out = refs[:n_casts], refs[n_casts], refs[n_casts + 1:]
    for src_ref, dst_ref in zip(cast_in, cast_out):
        dst_ref[...] = src_ref[...].astype(dst_ref.dtype)
    for r in range(0, x_ref.shape[0], IN_PROJ_NORM_ROWS):
        rows = pl.ds(r, IN_PROJ_NORM_ROWS)
        h = _rms_scale(x_ref[rows, :], gain_ref[...]).astype(BF16)
        o_ref[rows, :] = _dot(h, w_ref[...]).astype(o_ref.dtype)


def _matmul_kernel(h_ref, w_ref, o_ref):
    o_ref[...] = _dot(h_ref[...], w_ref[...]).astype(o_ref.dtype)


def _in_proj(x, gain, w, layer, *, tm, cast_along=()):
    m, d = x.shape
    n = w.shape[2]
    steps = m // tm
    rows = pl.BlockSpec((tm, d), lambda i: (i, 0))
    w_spec = pl.BlockSpec((None, d, n), lambda i: (layer, 0, 0), pipeline_mode=pl.Buffered(1))
    out_spec = pl.BlockSpec((tm, n), lambda i: (i, 0))
    out_shape = jax.ShapeDtypeStruct((m, n), BF16)
    if gain is None:
        assert not cast_along
        return pl.pallas_call(
            _matmul_kernel, grid=(steps,), in_specs=[rows, w_spec], out_specs=out_spec, out_shape=out_shape,
            compiler_params=_params("parallel"), name="in_proj")(x, w)
    views = [a.reshape(-1, a.shape[-1]) for a in cast_along]
    assert all(v.shape[0] % (steps * BF16_SUBLANES) == 0 for v in views)
    slabs = [pl.BlockSpec((v.shape[0] // steps, v.shape[1]), lambda i: (i, 0)) for v in views]
    out = pl.pallas_call(
        _norm_matmul_kernel,
        grid=(steps,),
        in_specs=[rows, pl.BlockSpec((1, d), lambda i: (0, 0)), w_spec] + slabs,
        out_specs=[out_spec] + slabs,
        out_shape=[out_shape] + [jax.ShapeDtypeStruct(v.shape, BF16) for v in views],
        compiler_params=_params("parallel"),
        name="norm_in_proj",
    )(x, gain.reshape(1, d), w, *views)
    return out[0], [o.reshape(a.shape) for o, a in zip(out[1:], cast_along)]


def _retention_tables():
    c = RET_CHUNK
    h = np.arange(RET_HEADS, dtype=np.float64)
    log_gamma = np.log(1.0 - 2.0 ** (-5.0 - h))
    idx = np.arange(c, dtype=np.float64)
    diff = idx[:, None] - idx[None, :]
    scale = HEAD_DIM ** -0.5
    decay = np.where(diff >= 0, np.exp(log_gamma[:, None, None] * np.maximum(diff, 0.0)), 0.0) * scale
    xi = np.exp(log_gamma[:, None] * (idx + 1.0)[None, :])
    zeta = np.exp(log_gamma[:, None] * (c - 1.0 - idx)[None, :]) * scale
    chunk_decay = np.exp(log_gamma * c)
    ones = np.ones((RET_HEADS, c, HEAD_DIM))
    t = np.stack([decay, xi[:, :, None] * ones, zeta[:, :, None] * ones, chunk_decay[:, None, None] * ones], axis=1)
    return jnp.asarray(t, dtype=F32)


def _retention_kernel(q_ref, k_ref, v_ref, g_ref, tab_ref, gn_ref, o_ref, state_ref):
    @pl.when(pl.program_id(2) == 0)
    def _():
        state_ref[...] = jnp.zeros_like(state_ref)

    c = RET_CHUNK
    decay = tab_ref[0, 0]
    xi = tab_ref[0, 1]
    zeta = tab_ref[0, 2]
    chunk_decay = tab_ref[0, 3]
    gn_w = gn_ref[0]

    chunks = [pl.ds(n * c, c) for n in range(q_ref.shape[0] // c)]
    scores = [(_dot_nt(q_ref[rows, :], k_ref[rows, :]) * decay).astype(BF16) for rows in chunks]
    contribs = [_dot_tn((k_ref[rows, :].astype(F32) * zeta).astype(BF16), v_ref[rows, :]) for rows in chunks]
    states = []
    state = state_ref[...]
    for contrib in contribs:
        states.append(state.astype(BF16))
        state = state * chunk_decay + contrib
    state_ref[...] = state
    outs = [_dot(s, v_ref[rows, :]) + _dot(q_ref[rows, :], st) * xi for rows, s, st in zip(chunks, scores, states)]
    for rows, o in zip(chunks, outs):
        mu = jnp.mean(o, axis=-1, keepdims=True)
        oc = o - mu
        var = jnp.mean(oc * oc, axis=-1, keepdims=True)
        on = oc * lax.rsqrt(var + GN_EPS) * gn_w
        g = g_ref[rows, :].astype(F32)
        o_ref[rows, :] = (g / (1.0 + jnp.exp(-g)) * on).astype(o_ref.dtype)


def _retention(proj, gn_w, batch, seq, *, rows):
    m = proj.shape[0]
    d = HEAD_DIM
    steps = seq // rows

    def col(base):
        return pl.BlockSpec((rows, d), lambda b, h, s: (b * steps + s, base + h))

    return pl.pallas_call(
        _retention_kernel,
        grid=(batch, RET_HEADS, steps),
        in_specs=[
            col(COL_RET_Q), col(COL_RET_K), col(COL_RET_V), col(COL_RET_G),
            pl.BlockSpec((1, 4, RET_CHUNK, d), lambda b, h, s: (h, 0, 0, 0)),
            pl.BlockSpec((1, 1, d), lambda b, h, s: (h, 0, 0)),
        ],
        out_specs=pl.BlockSpec((rows, d), lambda b, h, s: (b * steps + s, h)),
        out_shape=jax.ShapeDtypeStruct((m, D_RET), BF16),
        scratch_shapes=[pltpu.VMEM((d, d), F32)],
        compiler_params=_params("parallel", "parallel", "arbitrary"),
        name="retention",
    )(proj, proj, proj, proj, _retention_tables(), gn_w.reshape(RET_HEADS, 1, d))


SB_TILE = 256


def _sb_kernel(q_ref, k_ref, v_ref, tri_ref, o_ref, acc_ref, carry_ref):
    t = SB_TILE
    to_log2_logit = HEAD_DIM ** -0.5 * math.log2(math.e)
    first_block = pl.program_id(2) * (q_ref.shape[0] // t)

    def log_gates(q, keys):
        z = _dot_nt(q, k_ref[keys, :]) * to_log2_logit
        log_beta = jnp.minimum(z, 0.0) - jnp.log2(1.0 + jnp.exp2(-jnp.abs(z)))
        return log_beta, log_beta - z

    def later_keys_sum(log_keep):
        return _dot(log_keep.astype(BF16), tri_ref[...])

    causal = lax.broadcasted_iota(jnp.int32, (t, t), 1) < lax.broadcasted_iota(jnp.int32, (t, t), 0)
    tiles = range(q_ref.shape[0] // t)
    rows = [pl.ds(i * t, t) for i in tiles]
    near = [pl.ds(pl.multiple_of((first_block + i) * t, t), t) for i in tiles]
    far = [pl.ds(pl.multiple_of(jnp.maximum(first_block + i - 1, 0) * t, t), t) for i in tiles]
    has_far = [(first_block + i > 0).astype(F32) for i in tiles]
    gates_near = [log_gates(q_ref[rows[i], :], near[i]) for i in tiles]
    gates_far = [log_gates(q_ref[rows[i], :], far[i]) for i in tiles]
    keep_near = [jnp.where(causal, gates_near[i][1], 0.0) for i in tiles]
    later_near = [later_keys_sum(keep_near[i]) for i in tiles]
    later_far = [later_keys_sum(gates_far[i][1]) for i in tiles]
    carry_near = [jnp.sum(keep_near[i], axis=-1, keepdims=True) for i in tiles]
    a_near = [jnp.where(causal, jnp.exp2(gates_near[i][0] + later_near[i]), 0.0).astype(BF16) for i in tiles]
    a_far = [jnp.exp2(gates_far[i][0] + later_far[i] + carry_near[i]).astype(BF16) for i in tiles]
    for i in tiles:
        v_far = v_ref[far[i], :] * has_far[i].astype(BF16)
        acc_ref[rows[i], :] = _dot(a_near[i], v_ref[near[i], :]) + _dot(a_far[i], v_far)
        carry_ref[rows[i], :] = carry_near[i] + jnp.sum(gates_far[i][1], axis=-1, keepdims=True) * has_far[i]

    @pl.when(jnp.max(carry_ref[...]) > SB_LOG2_ZERO)
    def _():
        def sub_tile(i, carry):
            rows = pl.ds(pl.multiple_of(i * t, t), t)
            q = q_ref[rows, :]

            def cond(state):
                key_block, max_carry = state
                return jnp.logical_and(key_block >= 0, max_carry > SB_LOG2_ZERO)

            def body(state):
                key_block, _ = state
                keys = pl.ds(pl.multiple_of(key_block * t, t), t)
                log_beta, log_keep = log_gates(q, keys)
                a = jnp.exp2(log_beta + later_keys_sum(log_keep) + carry_ref[rows, :])
                acc_ref[rows, :] += _dot(a.astype(BF16), v_ref[keys, :])
                carry_ref[rows, :] += jnp.sum(log_keep, axis=-1, keepdims=True)
                return key_block - 1, jnp.max(carry_ref[rows, :])

            lax.while_loop(cond, body, (first_block + i - 2, jnp.max(carry_ref[rows, :])))
            return carry

        lax.fori_loop(0, q_ref.shape[0] // t, sub_tile, 0)

    o_ref[...] = acc_ref[...].astype(o_ref.dtype)


def _stick_breaking(proj, batch, seq, *, rows):
    m = proj.shape[0]
    d = HEAD_DIM
    t = SB_TILE
    steps = seq // rows
    idx = np.arange(t)
    tri = jnp.asarray(idx[:, None] > idx[None, :], dtype=BF16)
    return pl.pallas_call(
        _sb_kernel,
        grid=(batch, SB_HEADS, steps),
        in_specs=[
            pl.BlockSpec((rows, d), lambda b, h, s: (b * steps + s, COL_SB_Q + h)),
            pl.BlockSpec((seq, d), lambda b, h, s: (b, COL_SB_K + h)),
            pl.BlockSpec((seq, d), lambda b, h, s: (b, COL_SB_V + h)),
            pl.BlockSpec((t, t), lambda b, h, s: (0, 0)),
        ],
        out_specs=pl.BlockSpec((rows, d), lambda b, h, s: (b * steps + s, h)),
        out_shape=jax.ShapeDtypeStruct((m, D_SB), BF16),
        scratch_shapes=[pltpu.VMEM((rows, d), F32), pltpu.VMEM((rows, 1), F32)],
        compiler_params=_params("parallel", "parallel", "arbitrary"),
        name="stick_breaking",
    )(proj, proj, proj, tri)


def _swa_bias():
    w = WINDOW
    qi = np.arange(w)[:, None]
    kj = np.arange(2 * w)[None, :]
    dist = qi + w - kj
    valid = (dist >= 0) & (dist < w)
    slopes = 2.0 ** (-(8.0 / SWA_Q_HEADS) * (np.arange(SWA_Q_HEADS, dtype=np.float64) + 1.0))
    bias = np.where(valid[None], -slopes[:, None, None] * dist[None].astype(np.float64), -np.inf)
    return jnp.asarray(bias, dtype=F32)


def _swa_kernel(sink_ref, q_ref, kc_ref, vc_ref, kp_ref, vp_ref, bias_ref, o_ref, kb_ref, vb_ref):
    w = WINDOW
    d = HEAD_DIM
    scale = d ** -0.5
    rows = q_ref.shape[0]
    kv_head = pl.program_id(1)
    has_prev = pl.program_id(2) > 0
    kb_ref[pl.ds(0, w), :] = kp_ref[...]
    kb_ref[pl.ds(w, rows), :] = kc_ref[...]
    vb_ref[pl.ds(0, w), pl.ds(0, d)] = vp_ref[...]
    vb_ref[pl.ds(w, rows), pl.ds(0, d)] = vc_ref[...]
    vb_ref[:, pl.ds(d, d)] = jnp.ones((rows + w, d), BF16)
    bias = bias_ref[0]
    sinks = [sink_ref[kv_head * SWA_GROUP + g] for g in range(SWA_GROUP)]
    blocks = range(rows // w)
    groups = range(SWA_GROUP)
    heads = [slice(g * w, (g + 1) * w) for g in groups]
    scores = []
    for blk in blocks:
        tok = pl.ds(blk * w, w)
        q = jnp.concatenate([q_ref[tok, pl.ds(g * d, d)] for g in groups], axis=0)
        s = _dot_nt(q, kb_ref[pl.ds(blk * w, 2 * w), :]) * scale + bias
        if blk == 0:
            in_current = lax.broadcasted_iota(jnp.int32, s.shape, 1) >= w
            s = jnp.where(jnp.logical_or(in_current, has_prev), s, -jnp.inf)
        scores.append(s)
    row_max = [jnp.max(s, axis=-1, keepdims=True) for s in scores]
    mx = [[jnp.maximum(row_max[blk][heads[g]], sinks[g]) for g in groups] for blk in blocks]
    probs = [jnp.concatenate([jnp.exp(scores[blk][heads[g]] - mx[blk][g]).astype(BF16) for g in groups], axis=0)
             for blk in blocks]
    outs = [_dot(probs[blk], vb_ref[pl.ds(blk * w, 2 * w), :]) for blk in blocks]
    for blk in blocks:
        for g in groups:
            denom = outs[blk][heads[g], d:] + jnp.exp(sinks[g] - mx[blk][g])
            o_ref[pl.ds(blk * w, w), pl.ds(g * d, d)] = (outs[blk][heads[g], :d] / denom).astype(o_ref.dtype)


def _sliding_window(proj, sinks, batch, seq, *, rows):
    m = proj.shape[0]
    d = HEAD_DIM
    w = WINDOW
    steps = seq // rows
    blocks_per_step = rows // w
    blocks_per_seq = seq // w

    gw = SWA_GROUP * w

    def cur(base):
        return pl.BlockSpec((rows, d), lambda b, h, s, sink: (b * steps + s, base + h))

    def prev(base):
        return pl.BlockSpec(
            (w, d), lambda b, h, s, sink: (b * blocks_per_seq + jnp.maximum(s * blocks_per_step - 1, 0), base + h))

    grid_spec = pltpu.PrefetchScalarGridSpec(
        num_scalar_prefetch=1,
        grid=(batch, SWA_KV_HEADS, steps),
        in_specs=[
            pl.BlockSpec((rows, SWA_GROUP * d), lambda b, h, s, sink: (b * steps + s, COL_SWA_Q // SWA_GROUP + h)),
            cur(COL_SWA_K), cur(COL_SWA_V), prev(COL_SWA_K), prev(COL_SWA_V),
            pl.BlockSpec((1, gw, 2 * w), lambda b, h, s, sink: (h, 0, 0)),
        ],
        out_specs=pl.BlockSpec((rows, SWA_GROUP * d), lambda b, h, s, sink: (b * steps + s, h)),
        scratch_shapes=[pltpu.VMEM((rows + w, d), BF16), pltpu.VMEM((rows + w, 2 * d), BF16)],
    )
    return pl.pallas_call(
        _swa_kernel,
        grid_spec=grid_spec,
        out_shape=jax.ShapeDtypeStruct((m, D_SWA), BF16),
        compiler_params=_params("parallel", "parallel", "arbitrary"),
        name="sliding_window",
    )(sinks.astype(F32), proj, proj, proj, proj, proj, _swa_bias().reshape(SWA_KV_HEADS, gw, 2 * w))


EPILOGUE_ROWS = 128


def _row_chunks(rows):
    size = min(rows, EPILOGUE_ROWS)
    return [pl.ds(r, size) for r in range(0, rows, size)]


def _out_proj_kernel(ret_ref, sb_ref, swa_ref, w_ref, gain_ref, next_gain_ref, x_ref, o_ref, h_ref):
    for rows in _row_chunks(x_ref.shape[0]):
        y = _dot(ret_ref[rows, :], w_ref[pl.ds(0, D_RET), :])
        y += _dot(sb_ref[rows, :], w_ref[pl.ds(D_RET, D_SB), :])
        y += _dot(swa_ref[rows, :], w_ref[pl.ds(D_RET + D_SB, D_SWA), :])
        x = x_ref[rows, :] + _rms_scale(y, gain_ref[...])
        o_ref[rows, :] = x
        h_ref[rows, :] = _rms_scale(x, next_gain_ref[...]).astype(h_ref.dtype)


def _out_proj(o_ret, o_sb, o_swa, w, layer, gain, next_gain, x, *, tm):
    m, d = x.shape

    def rows(width):
        return pl.BlockSpec((tm, width), lambda i: (i, 0))

    vec = pl.BlockSpec((1, d), lambda i: (0, 0))
    return pl.pallas_call(
        _out_proj_kernel,
        grid=(m // tm,),
        in_specs=[
            rows(D_RET), rows(D_SB), rows(D_SWA),
            pl.BlockSpec((None,) + w.shape[1:], lambda i: (layer, 0, 0)),
            vec, vec, rows(d),
        ],
        out_specs=[rows(d), rows(d)],
        out_shape=[jax.ShapeDtypeStruct((m, d), F32), jax.ShapeDtypeStruct((m, d), BF16)],
        compiler_params=_params("parallel"),
        name="out_proj_norm_res",
    )(o_ret, o_sb, o_swa, w, gain.reshape(1, d), next_gain.reshape(1, d), x)


TAIL = 8
assert TAIL >= CONV_WIDTH - 1
FFN_UP_HALVES = 2
FFN_UP_CHUNKS = (1792, 1024)
FFN_UP_ROWS = 512


def _gelu_tanh(x):
    return 0.5 * x * (1.0 + jnp.tanh(math.sqrt(2.0 / math.pi) * (x + 0.044715 * (x * x * x))))


def _ffn_up_kernel(h_ref, wa_ref, wb_ref, cw_ref, cb_ref, o_ref, a_ref, b_ref, tail_ref, *, tiles_per_seq):
    tm = h_ref.shape[0]
    starts_sequence = pl.program_id(1) % tiles_per_seq == 0

    @pl.when(starts_sequence)
    def _():
        a_ref[pl.ds(0, TAIL), :] = jnp.zeros((TAIL, a_ref.shape[1]), F32)

    @pl.when(jnp.logical_not(starts_sequence))
    def _():
        a_ref[pl.ds(0, TAIL), :] = tail_ref[...]

    tn = a_ref.shape[1]
    assert sum(FFN_UP_CHUNKS) == tn
    chunks = [pl.ds(sum(FFN_UP_CHUNKS[:c]), width) for c, width in enumerate(FFN_UP_CHUNKS)]

    def matmuls(cols):
        a_ref[pl.ds(TAIL, tm), cols] = _dot(h_ref[...], wa_ref[:, cols])
        b_ref[:, cols] = _dot(h_ref[...], wb_ref[:, cols])

    def conv_gate(cols):
        for r in range(0, tm, FFN_UP_ROWS):
            a_conv = cb_ref[:, cols]
            for tap in range(CONV_WIDTH):
                taps = pl.ds(r + TAIL - (CONV_WIDTH - 1) + tap, FFN_UP_ROWS)
                a_conv = a_conv + a_ref[taps, cols] * cw_ref[pl.ds(tap, 1), cols]
            rows = pl.ds(r, FFN_UP_ROWS)
            o_ref[rows, cols] = (_gelu_tanh(a_conv) * b_ref[rows, cols]).astype(o_ref.dtype)

    matmuls(chunks[0])
    for c, cols in enumerate(chunks):
        if c + 1 < len(chunks):
            matmuls(chunks[c + 1])
        conv_gate(cols)
    tail_ref[...] = a_ref[pl.ds(tm, TAIL), :]


def _ffn_up(h, w_up, layer, conv_w, conv_b, seq, *, tm):
    m, d = h.shape
    f = conv_w.shape[1]
    tn = f // FFN_UP_HALVES
    resident = dict(pipeline_mode=pl.Buffered(1))
    return pl.pallas_call(
        functools.partial(_ffn_up_kernel, tiles_per_seq=seq // tm),
        grid=(FFN_UP_HALVES, m // tm),
        in_specs=[
            pl.BlockSpec((tm, d), lambda j, i: (i, 0)),
            pl.BlockSpec((None, d, tn), lambda j, i: (layer, 0, j), **resident),
            pl.BlockSpec((None, d, tn), lambda j, i: (layer, 0, FFN_UP_HALVES + j), **resident),
            pl.BlockSpec((CONV_WIDTH, tn), lambda j, i: (0, j)),
            pl.BlockSpec((1, tn), lambda j, i: (0, j)),
        ],
        out_specs=pl.BlockSpec((tm, tn), lambda j, i: (i, j)),
        out_shape=jax.ShapeDtypeStruct((m, f), BF16),
        scratch_shapes=[pltpu.VMEM((TAIL + tm, tn), F32), pltpu.VMEM((tm, tn), F32), pltpu.VMEM((TAIL, tn), F32)],
        compiler_params=_params("arbitrary", "arbitrary"),
        name="ffn_up_conv_gate",
    )(h, w_up, w_up, conv_w, conv_b.reshape(1, f))


def _ffn_down_kernel(g_ref, w_ref, gain_ref, next_gain_ref, x_ref, o_ref, *maybe_h_ref):
    for rows in _row_chunks(x_ref.shape[0]):
        x = x_ref[rows, :] + _rms_scale(_dot(g_ref[rows, :], w_ref[...]), gain_ref[...])
        o_ref[rows, :] = x
        for h_ref in maybe_h_ref:
            h_ref[rows, :] = _rms_scale(x, next_gain_ref[...]).astype(h_ref.dtype)


def _ffn_down(g, w, layer, gain, next_gain, x, *, tm):
    m, d = x.shape
    f = g.shape[1]
    rows = pl.BlockSpec((tm, d), lambda i: (i, 0))
    vec = pl.BlockSpec((1, d), lambda i: (0, 0))
    emit_next = next_gain is not None
    out = pl.pallas_call(
        _ffn_down_kernel,
        grid=(m // tm,),
        in_specs=[
            pl.BlockSpec((tm, f), lambda i: (i, 0)),
            pl.BlockSpec((None, f, d), lambda i: (layer, 0, 0), pipeline_mode=pl.Buffered(1)),
            vec, vec, rows,
        ],
        out_specs=[rows, rows] if emit_next else [rows],
        out_shape=[jax.ShapeDtypeStruct((m, d), F32)] + ([jax.ShapeDtypeStruct((m, d), BF16)] if emit_next else []),
        compiler_params=_params("parallel"),
        name="ffn_down_norm_res",
    )(g, w, gain.reshape(1, d), (next_gain if emit_next else gain).reshape(1, d), x)
    return (out[0], out[1]) if emit_next else (out[0], None)


def _tile(extent, preferred):
    t = min(extent, preferred)
    while extent % t:
        t //= 2
    return t


def kernel(x, w_in, w_out, ret_gn_w, swa_sinks, norm_mix_pre, norm_mix_post, norm_ffn_pre, norm_ffn_post, w_up, conv_w, conv_b, w_down):
    batch, seq, d_model = x.shape
    depth = w_in.shape[0]
    d_ff = conv_w.shape[-1]
    assert seq % SB_TILE == 0 and d_ff % 512 == 0
    xf = x.reshape(batch * seq, d_model).astype(F32)
    w_in, w_out = w_in.astype(BF16), w_out.astype(BF16)
    tm_small = _tile(seq, 512)
    h = None
    for l in range(depth):
        if h is None:
            proj, (w_up, w_down) = _in_proj(xf, norm_mix_pre[l], w_in, l, tm=tm_small, cast_along=(w_up, w_down))
        else:
            proj = _in_proj(h, None, w_in, l, tm=tm_small)
        o_ret = _retention(proj, ret_gn_w[l], batch, seq, rows=_tile(seq, 4096))
        o_sb = _stick_breaking(proj, batch, seq, rows=_tile(seq, 2048))
        o_swa = _sliding_window(proj, swa_sinks[l], batch, seq, rows=_tile(seq, 2048))
        xf, h = _out_proj(o_ret, o_sb, o_swa, w_out, l, norm_mix_post[l], norm_ffn_pre[l], xf, tm=tm_small)
        g = _ffn_up(h, w_up, l, conv_w[l], conv_b[l], seq, tm=tm_small)
        next_gain = norm_mix_pre[l + 1] if l + 1 < depth else None
        xf, h = _ffn_down(g, w_down, l, norm_ffn_post[l], next_gain, xf, tm=tm_small)
    return xf.reshape(batch, seq, d_model).astype(x.dtype)
```

```python
import functools
import math

import jax
import jax.numpy as jnp
import numpy as np
from jax import lax
from jax.experimental import pallas as pl
from jax.experimental.pallas import tpu as pltpu

HEAD_DIM = 128
RET_HEADS = 4
SB_HEADS = 4
SWA_Q_HEADS = 8
SWA_KV_HEADS = 2
SWA_GROUP = SWA_Q_HEADS // SWA_KV_HEADS
RET_CHUNK = 128
WINDOW = 128
CONV_WIDTH = 3
RMS_EPS = 1e-6
GN_EPS = 1e-5

COL_RET_Q, COL_RET_K, COL_RET_V, COL_RET_G = 0, 4, 8, 12
COL_SB_Q, COL_SB_K, COL_SB_V = 16, 20, 24
COL_SWA_Q, COL_SWA_K, COL_SWA_V = 28, 36, 38
D_RET = RET_HEADS * HEAD_DIM
D_SB = SB_HEADS * HEAD_DIM
D_SWA = SWA_Q_HEADS * HEAD_DIM

VMEM_LIMIT_BYTES = 56 * 1024 * 1024

SB_LOG2_ZERO = -151.0

F32 = jnp.float32
BF16 = jnp.bfloat16


def _params(*semantics):
    return pltpu.CompilerParams(dimension_semantics=semantics, vmem_limit_bytes=VMEM_LIMIT_BYTES)


def _rms_scale(x, gain):
    ms = jnp.mean(x * x, axis=-1, keepdims=True)
    return x * lax.rsqrt(ms + RMS_EPS) * gain


def _dot(a, b):
    return jnp.dot(a, b, preferred_element_type=F32)


def _dot_nt(a, b):
    return lax.dot_general(a, b, (((1,), (1,)), ((), ())), preferred_element_type=F32)


def _dot_tn(a, b):
    return lax.dot_general(a, b, (((0,), (0,)), ((), ())), preferred_element_type=F32)


IN_PROJ_NORM_ROWS = 128
BF16_SUBLANES = 16


def _norm_matmul_kernel(x_ref, gain_ref, w_ref, *refs):
    n_casts = (len(refs) - 1) // 2
    cast_in, o_ref, cast_out = refs[:n_casts], refs[n_casts], refs[n_casts + 1:]
    for src_ref, dst_ref in zip(cast_in, cast_out):
        dst_ref[...] = src_ref[...].astype(dst_ref.dtype)
    for r in range(0, x_ref.shape[0], IN_PROJ_NORM_ROWS):
        rows = pl.ds(r, IN_PROJ_NORM_ROWS)
        h = _rms_scale(x_ref[rows, :], gain_ref[...]).astype(BF16)
        o_ref[rows, :] = _dot(h, w_ref[...]).astype(o_ref.dtype)


def _matmul_kernel(h_ref, w_ref, o_ref):
    o_ref[...] = _dot(h_ref[...], w_ref[...]).astype(o_ref.dtype)


def _in_proj(x, gain, w, layer, *, tm, cast_along=()):
    m, d = x.shape
    n = w.shape[2]
    steps = m // tm
    rows = pl.BlockSpec((tm, d), lambda i: (i, 0))
    w_spec = pl.BlockSpec((None, d, n), lambda i: (layer, 0, 0), pipeline_mode=pl.Buffered(1))
    out_spec = pl.BlockSpec((tm, n), lambda i: (i, 0))
    out_shape = jax.ShapeDtypeStruct((m, n), BF16)
    if gain is None:
        assert not cast_along
        return pl.pallas_call(
            _matmul_kernel, grid=(steps,), in_specs=[rows, w_spec], out_specs=out_spec, out_shape=out_shape,
            compiler_params=_params("parallel"), name="in_proj")(x, w)
    views = [a.reshape(-1, a.shape[-1]) for a in cast_along]
    assert all(v.shape[0] % (steps * BF16_SUBLANES) == 0 for v in views)
    slabs = [pl.BlockSpec((v.shape[0] // steps, v.shape[1]), lambda i: (i, 0)) for v in views]
    out = pl.pallas_call(
        _norm_matmul_kernel,
        grid=(steps,),
        in_specs=[rows, pl.BlockSpec((1, d), lambda i: (0, 0)), w_spec] + slabs,
        out_specs=[out_spec] + slabs,
        out_shape=[out_shape] + [jax.ShapeDtypeStruct(v.shape, BF16) for v in views],
        compiler_params=_params("parallel"),
        name="norm_in_proj",
    )(x, gain.reshape(1, d), w, *views)
    return out[0], [o.reshape(a.shape) for o, a in zip(out[1:], cast_along)]


def _retention_tables():
    c = RET_CHUNK
    h = np.arange(RET_HEADS, dtype=np.float64)
    log_gamma = np.log(1.0 - 2.0 ** (-5.0 - h))
    idx = np.arange(c, dtype=np.float64)
    diff = idx[:, None] - idx[None, :]
    scale = HEAD_DIM ** -0.5
    decay = np.where(diff >= 0, np.exp(log_gamma[:, None, None] * np.maximum(diff, 0.0)), 0.0) * scale
    xi = np.exp(log_gamma[:, None] * (idx + 1.0)[None, :])
    zeta = np.exp(log_gamma[:, None] * (c - 1.0 - idx)[None, :]) * scale
    chunk_decay = np.exp(log_gamma * c)
    ones = np.ones((RET_HEADS, c, HEAD_DIM))
    t = np.stack([decay, xi[:, :, None] * ones, zeta[:, :, None] * ones, chunk_decay[:, None, None] * ones], axis=1)
    return jnp.asarray(t, dtype=F32)


def _retention_kernel(q_ref, k_ref, v_ref, g_ref, tab_ref, gn_ref, o_ref, state_ref):
    @pl.when(pl.program_id(2) == 0)
    def _():
        state_ref[...] = jnp.zeros_like(state_ref)

    c = RET_CHUNK
    decay = tab_ref[0, 0]
    xi = tab_ref[0, 1]
    zeta = tab_ref[0, 2]
    chunk_decay = tab_ref[0, 3]
    gn_w = gn_ref[0]

    chunks = [pl.ds(n * c, c) for n in range(q_ref.shape[0] // c)]
    scores = [(_dot_nt(q_ref[rows, :], k_ref[rows, :]) * decay).astype(BF16) for rows in chunks]
    contribs = [_dot_tn((k_ref[rows, :].astype(F32) * zeta).astype(BF16), v_ref[rows, :]) for rows in chunks]
    states = []
    state = state_ref[...]
    for contrib in contribs:
        states.append(state.astype(BF16))
        state = state * chunk_decay + contrib
    state_ref[...] = state
    outs = [_dot(s, v_ref[rows, :]) + _dot(q_ref[rows, :], st) * xi for rows, s, st in zip(chunks, scores, states)]
    for rows, o in zip(chunks, outs):
        mu = jnp.mean(o, axis=-1, keepdims=True)
        oc = o - mu
        var = jnp.mean(oc * oc, axis=-1, keepdims=True)
        on = oc * lax.rsqrt(var + GN_EPS) * gn_w
        g = g_ref[rows, :].astype(F32)
        o_ref[rows, :] = (g / (1.0 + jnp.exp(-g)) * on).astype(o_ref.dtype)


def _retention(proj, gn_w, batch, seq, *, rows):
    m = proj.shape[0]
    d = HEAD_DIM
    steps = seq // rows

    def col(base):
        return pl.BlockSpec((rows, d), lambda b, h, s: (b * steps + s, base + h))

    return pl.pallas_call(
        _retention_kernel,
        grid=(batch, RET_HEADS, steps),
        in_specs=[
            col(COL_RET_Q), col(COL_RET_K), col(COL_RET_V), col(COL_RET_G),
            pl.BlockSpec((1, 4, RET_CHUNK, d), lambda b, h, s: (h, 0, 0, 0)),
            pl.BlockSpec((1, 1, d), lambda b, h, s: (h, 0, 0)),
        ],
        out_specs=pl.BlockSpec((rows, d), lambda b, h, s: (b * steps + s, h)),
        out_shape=jax.ShapeDtypeStruct((m, D_RET), BF16),
        scratch_shapes=[pltpu.VMEM((d, d), F32)],
        compiler_params=_params("parallel", "parallel", "arbitrary"),
        name="retention",
    )(proj, proj, proj, proj, _retention_tables(), gn_w.reshape(RET_HEADS, 1, d))


SB_TILE = 256


def _sb_kernel(q_ref, k_ref, v_ref, tri_ref, o_ref, acc_ref, carry_ref):
    t = SB_TILE
    to_log2_logit = HEAD_DIM ** -0.5 * math.log2(math.e)
    first_block = pl.program_id(2) * (q_ref.shape[0] // t)

    def log_gates(q, keys):
        z = _dot_nt(q, k_ref[keys, :]) * to_log2_logit
        log_beta = jnp.minimum(z, 0.0) - jnp.log2(1.0 + jnp.exp2(-jnp.abs(z)))
        return log_beta, log_beta - z

    def later_keys_sum(log_keep):
        return _dot(log_keep.astype(BF16), tri_ref[...])

    causal = lax.broadcasted_iota(jnp.int32, (t, t), 1) < lax.broadcasted_iota(jnp.int32, (t, t), 0)
    tiles = range(q_ref.shape[0] // t)
    rows = [pl.ds(i * t, t) for i in tiles]
    near = [pl.ds(pl.multiple_of((first_block + i) * t, t), t) for i in tiles]
    far = [pl.ds(pl.multiple_of(jnp.maximum(first_block + i - 1, 0) * t, t), t) for i in tiles]
    has_far = [(first_block + i > 0).astype(F32) for i in tiles]
    gates_near = [log_gates(q_ref[rows[i], :], near[i]) for i in tiles]
    gates_far = [log_gates(q_ref[rows[i], :], far[i]) for i in tiles]
    keep_near = [jnp.where(causal, gates_near[i][1], 0.0) for i in tiles]
    later_near = [later_keys_sum(keep_near[i]) for i in tiles]
    later_far = [later_keys_sum(gates_far[i][1]) for i in tiles]
    carry_near = [jnp.sum(keep_near[i], axis=-1, keepdims=True) for i in tiles]
    a_near = [jnp.where(causal, jnp.exp2(gates_near[i][0] + later_near[i]), 0.0).astype(BF16) for i in tiles]
    a_far = [jnp.exp2(gates_far[i][0] + later_far[i] + carry_near[i]).astype(BF16) for i in tiles]
    for i in tiles:
        v_far = v_ref[far[i], :] * has_far[i].astype(BF16)
        acc_ref[rows[i], :] = _dot(a_near[i], v_ref[near[i], :]) + _dot(a_far[i], v_far)
        carry_ref[rows[i], :] = carry_near[i] + jnp.sum(gates_far[i][1], axis=-1, keepdims=True) * has_far[i]

    @pl.when(jnp.max(carry_ref[...]) > SB_LOG2_ZERO)
    def _():
        def sub_tile(i, carry):
            rows = pl.ds(pl.multiple_of(i * t, t), t)
            q = q_ref[rows, :]

            def cond(state):
                key_block, max_carry = state
                return jnp.logical_and(key_block >= 0, max_carry > SB_LOG2_ZERO)

            def body(state):
                key_block, _ = state
                keys = pl.ds(pl.multiple_of(key_block * t, t), t)
                log_beta, log_keep = log_gates(q, keys)
                a = jnp.exp2(log_beta + later_keys_sum(log_keep) + carry_ref[rows, :])
                acc_ref[rows, :] += _dot(a.astype(BF16), v_ref[keys, :])
                carry_ref[rows, :] += jnp.sum(log_keep, axis=-1, keepdims=True)
                return key_block - 1, jnp.max(carry_ref[rows, :])

            lax.while_loop(cond, body, (first_block + i - 2, jnp.max(carry_ref[rows, :])))
            return carry

        lax.fori_loop(0, q_ref.shape[0] // t, sub_tile, 0)

    o_ref[...] = acc_ref[...].astype(o_ref.dtype)


def _stick_breaking(proj, batch, seq, *, rows):
    m = proj.shape[0]
    d = HEAD_DIM
    t = SB_TILE
    steps = seq // rows
    idx = np.arange(t)
    tri = jnp.asarray(idx[:, None] > idx[None, :], dtype=BF16)
    return pl.pallas_call(
        _sb_kernel,
        grid=(batch, SB_HEADS, steps),
        in_specs=[
            pl.BlockSpec((rows, d), lambda b, h, s: (b * steps + s, COL_SB_Q + h)),
            pl.BlockSpec((seq, d), lambda b, h, s: (b, COL_SB_K + h)),
            pl.BlockSpec((seq, d), lambda b, h, s: (b, COL_SB_V + h)),
            pl.BlockSpec((t, t), lambda b, h, s: (0, 0)),
        ],
        out_specs=pl.BlockSpec((rows, d), lambda b, h, s: (b * steps + s, h)),
        out_shape=jax.ShapeDtypeStruct((m, D_SB), BF16),
        scratch_shapes=[pltpu.VMEM((rows, d), F32), pltpu.VMEM((rows, 1), F32)],
        compiler_params=_params("parallel", "parallel", "arbitrary"),
        name="stick_breaking",
    )(proj, proj, proj, tri)


def _swa_bias():
    w = WINDOW
    qi = np.arange(w)[:, None]
    kj = np.arange(2 * w)[None, :]
    dist = qi + w - kj
    valid = (dist >= 0) & (dist < w)
    slopes = 2.0 ** (-(8.0 / SWA_Q_HEADS) * (np.arange(SWA_Q_HEADS, dtype=np.float64) + 1.0))
    bias = np.where(valid[None], -slopes[:, None, None] * dist[None].astype(np.float64), -np.inf)
    return jnp.asarray(bias, dtype=F32)


def _swa_kernel(sink_ref, q_ref, kc_ref, vc_ref, kp_ref, vp_ref, bias_ref, o_ref, kb_ref, vb_ref):
    w = WINDOW
    d = HEAD_DIM
    scale = d ** -0.5
    rows = q_ref.shape[0]
    kv_head = pl.program_id(1)
    has_prev = pl.program_id(2) > 0
    kb_ref[pl.ds(0, w), :] = kp_ref[...]
    kb_ref[pl.ds(w, rows), :] = kc_ref[...]
    vb_ref[pl.ds(0, w), pl.ds(0, d)] = vp_ref[...]
    vb_ref[pl.ds(w, rows), pl.ds(0, d)] = vc_ref[...]
    vb_ref[:, pl.ds(d, d)] = jnp.ones((rows + w, d), BF16)
    bias = bias_ref[0]
    sinks = [sink_ref[kv_head * SWA_GROUP + g] for g in range(SWA_GROUP)]
    blocks = range(rows // w)
    groups = range(SWA_GROUP)
    heads = [slice(g * w, (g + 1) * w) for g in groups]
    scores = []
    for blk in blocks:
        tok = pl.ds(blk * w, w)
        q = jnp.concatenate([q_ref[tok, pl.ds(g * d, d)] for g in groups], axis=0)
        s = _dot_nt(q, kb_ref[pl.ds(blk * w, 2 * w), :]) * scale + bias
        if blk == 0:
            in_current = lax.broadcasted_iota(jnp.int32, s.shape, 1) >= w
            s = jnp.where(jnp.logical_or(in_current, has_prev), s, -jnp.inf)
        scores.append(s)
    row_max = [jnp.max(s, axis=-1, keepdims=True) for s in scores]
    mx = [[jnp.maximum(row_max[blk][heads[g]], sinks[g]) for g in groups] for blk in blocks]
    probs = [jnp.concatenate([jnp.exp(scores[blk][heads[g]] - mx[blk][g]).astype(BF16) for g in groups], axis=0)
             for blk in blocks]
    outs = [_dot(probs[blk], vb_ref[pl.ds(blk * w, 2 * w), :]) for blk in blocks]
    for blk in blocks:
        for g in groups:
            denom = outs[blk][heads[g], d:] + jnp.exp(sinks[g] - mx[blk][g])
            o_ref[pl.ds(blk * w, w), pl.ds(g * d, d)] = (outs[blk][heads[g], :d] / denom).astype(o_ref.dtype)


def _sliding_window(proj, sinks, batch, seq, *, rows):
    m = proj.shape[0]
    d = HEAD_DIM
    w = WINDOW
    steps = seq // rows
    blocks_per_step = rows // w
    blocks_per_seq = seq // w

    gw = SWA_GROUP * w

    def cur(base):
        return pl.BlockSpec((rows, d), lambda b, h, s, sink: (b * steps + s, base + h))

    def prev(base):
        return pl.BlockSpec(
            (w, d), lambda b, h, s, sink: (b * blocks_per_seq + jnp.maximum(s * blocks_per_step - 1, 0), base + h))

    grid_spec = pltpu.PrefetchScalarGridSpec(
        num_scalar_prefetch=1,
        grid=(batch, SWA_KV_HEADS, steps),
        in_specs=[
            pl.BlockSpec((rows, SWA_GROUP * d), lambda b, h, s, sink: (b * steps + s, COL_SWA_Q // SWA_GROUP + h)),
            cur(COL_SWA_K), cur(COL_SWA_V), prev(COL_SWA_K), prev(COL_SWA_V),
            pl.BlockSpec((1, gw, 2 * w), lambda b, h, s, sink: (h, 0, 0)),
        ],
        out_specs=pl.BlockSpec((rows, SWA_GROUP * d), lambda b, h, s, sink: (b * steps + s, h)),
        scratch_shapes=[pltpu.VMEM((rows + w, d), BF16), pltpu.VMEM((rows + w, 2 * d), BF16)],
    )
    return pl.pallas_call(
        _swa_kernel,
        grid_spec=grid_spec,
        out_shape=jax.ShapeDtypeStruct((m, D_SWA), BF16),
        compiler_params=_params("parallel", "parallel", "arbitrary"),
        name="sliding_window",
    )(sinks.astype(F32), proj, proj, proj, proj, proj, _swa_bias().reshape(SWA_KV_HEADS, gw, 2 * w))


EPILOGUE_ROWS = 128


def _row_chunks(rows):
    size = min(rows, EPILOGUE_ROWS)
    return [pl.ds(r, size) for r in range(0, rows, size)]


def _out_proj_kernel(ret_ref, sb_ref, swa_ref, w_ref, gain_ref, next_gain_ref, x_ref, o_ref, h_ref):
    for rows in _row_chunks(x_ref.shape[0]):
        y = _dot(ret_ref[rows, :], w_ref[pl.ds(0, D_RET), :])
        y += _dot(sb_ref[rows, :], w_ref[pl.ds(D_RET, D_SB), :])
        y += _dot(swa_ref[rows, :], w_ref[pl.ds(D_RET + D_SB, D_SWA), :])
        x = x_ref[rows, :] + _rms_scale(y, gain_ref[...])
        o_ref[rows, :] = x
        h_ref[rows, :] = _rms_scale(x, next_gain_ref[...]).astype(h_ref.dtype)


def _out_proj(o_ret, o_sb, o_swa, w, layer, gain, next_gain, x, *, tm):
    m, d = x.shape

    def rows(width):
        return pl.BlockSpec((tm, width), lambda i: (i, 0))

    vec = pl.BlockSpec((1, d), lambda i: (0, 0))
    return pl.pallas_call(
        _out_proj_kernel,
        grid=(m // tm,),
        in_specs=[
            rows(D_RET), rows(D_SB), rows(D_SWA),
            pl.BlockSpec((None,) + w.shape[1:], lambda i: (layer, 0, 0)),
            vec, vec, rows(d),
        ],
        out_specs=[rows(d), rows(d)],
        out_shape=[jax.ShapeDtypeStruct((m, d), F32), jax.ShapeDtypeStruct((m, d), BF16)],
        compiler_params=_params("parallel"),
        name="out_proj_norm_res",
    )(o_ret, o_sb, o_swa, w, gain.reshape(1, d), next_gain.reshape(1, d), x)


TAIL = 8
assert TAIL >= CONV_WIDTH - 1
FFN_UP_HALVES = 2
FFN_UP_CHUNKS = (1792, 1024)
FFN_UP_ROWS = 512


def _gelu_tanh(x):
    return 0.5 * x * (1.0 + jnp.tanh(math.sqrt(2.0 / math.pi) * (x + 0.044715 * (x * x * x))))


def _ffn_up_kernel(h_ref, wa_ref, wb_ref, cw_ref, cb_ref, o_ref, a_ref, b_ref, tail_ref, *, tiles_per_seq):
    tm = h_ref.shape[0]
    starts_sequence = pl.program_id(1) % tiles_per_seq == 0

    @pl.when(starts_sequence)
    def _():
        a_ref[pl.ds(0, TAIL), :] = jnp.zeros((TAIL, a_ref.shape[1]), F32)

    @pl.when(jnp.logical_not(starts_sequence))
    def _():
        a_ref[pl.ds(0, TAIL), :] = tail_ref[...]

    tn = a_ref.shape[1]
    assert sum(FFN_UP_CHUNKS) == tn
    chunks = [pl.ds(sum(FFN_UP_CHUNKS[:c]), width) for c, width in enumerate(FFN_UP_CHUNKS)]

    def matmuls(cols):
        a_ref[pl.ds(TAIL, tm), cols] = _dot(h_ref[...], wa_ref[:, cols])
        b_ref[:, cols] = _dot(h_ref[...], wb_ref[:, cols])

    def conv_gate(cols):
        for r in range(0, tm, FFN_UP_ROWS):
            a_conv = cb_ref[:, cols]
            for tap in range(CONV_WIDTH):
                taps = pl.ds(r + TAIL - (CONV_WIDTH - 1) + tap, FFN_UP_ROWS)
                a_conv = a_conv + a_ref[taps, cols] * cw_ref[pl.ds(tap, 1), cols]
            rows = pl.ds(r, FFN_UP_ROWS)
            o_ref[rows, cols] = (_gelu_tanh(a_conv) * b_ref[rows, cols]).astype(o_ref.dtype)

    matmuls(chunks[0])
    for c, cols in enumerate(chunks):
        if c + 1 < len(chunks):
            matmuls(chunks[c + 1])
        conv_gate(cols)
    tail_ref[...] = a_ref[pl.ds(tm, TAIL), :]


def _ffn_up(h, w_up, layer, conv_w, conv_b, seq, *, tm):
    m, d = h.shape
    f = conv_w.shape[1]
    tn = f // FFN_UP_HALVES
    resident = dict(pipeline_mode=pl.Buffered(1))
    return pl.pallas_call(
        functools.partial(_ffn_up_kernel, tiles_per_seq=seq // tm),
        grid=(FFN_UP_HALVES, m // tm),
        in_specs=[
            pl.BlockSpec((tm, d), lambda j, i: (i, 0)),
            pl.BlockSpec((None, d, tn), lambda j, i: (layer, 0, j), **resident),
            pl.BlockSpec((None, d, tn), lambda j, i: (layer, 0, FFN_UP_HALVES + j), **resident),
            pl.BlockSpec((CONV_WIDTH, tn), lambda j, i: (0, j)),
            pl.BlockSpec((1, tn), lambda j, i: (0, j)),
        ],
        out_specs=pl.BlockSpec((tm, tn), lambda j, i: (i, j)),
        out_shape=jax.ShapeDtypeStruct((m, f), BF16),
        scratch_shapes=[pltpu.VMEM((TAIL + tm, tn), F32), pltpu.VMEM((tm, tn), F32), pltpu.VMEM((TAIL, tn), F32)],
        compiler_params=_params("arbitrary", "arbitrary"),
        name="ffn_up_conv_gate",
    )(h, w_up, w_up, conv_w, conv_b.reshape(1, f))


def _ffn_down_kernel(g_ref, w_ref, gain_ref, next_gain_ref, x_ref, o_ref, *maybe_h_ref):
    for rows in _row_chunks(x_ref.shape[0]):
        x = x_ref[rows, :] + _rms_scale(_dot(g_ref[rows, :], w_ref[...]), gain_ref[...])
        o_ref[rows, :] = x
        for h_ref in maybe_h_ref:
            h_ref[rows, :] = _rms_scale(x, next_gain_ref[...]).astype(h_ref.dtype)


def _ffn_down(g, w, layer, gain, next_gain, x, *, tm):
    m, d = x.shape
    f = g.shape[1]
    rows = pl.BlockSpec((tm, d), lambda i: (i, 0))
    vec = pl.BlockSpec((1, d), lambda i: (0, 0))
    emit_next = next_gain is not None
    out = pl.pallas_call(
        _ffn_down_kernel,
        grid=(m // tm,),
        in_specs=[
            pl.BlockSpec((tm, f), lambda i: (i, 0)),
            pl.BlockSpec((None, f, d), lambda i: (layer, 0, 0), pipeline_mode=pl.Buffered(1)),
            vec, vec, rows,
        ],
        out_specs=[rows, rows] if emit_next else [rows],
        out_shape=[jax.ShapeDtypeStruct((m, d), F32)] + ([jax.ShapeDtypeStruct((m, d), BF16)] if emit_next else []),
        compiler_params=_params("parallel"),
        name="ffn_down_norm_res",
    )(g, w, gain.reshape(1, d), (next_gain if emit_next else gain).reshape(1, d), x)
    return (out[0], out[1]) if emit_next else (out[0], None)


def _tile(extent, preferred):
    t = min(extent, preferred)
    while extent % t:
        t //= 2
    return t


def kernel(x, w_in, w_out, ret_gn_w, swa_sinks, norm_mix_pre, norm_mix_post, norm_ffn_pre, norm_ffn_post, w_up, conv_w, conv_b, w_down):
    batch, seq, d_model = x.shape
    depth = w_in.shape[0]
    d_ff = conv_w.shape[-1]
    assert seq % SB_TILE == 0 and d_ff % 512 == 0
    xf = x.reshape(batch * seq, d_model).astype(F32)
    w_in = w_in.astype(BF16)
    tm_small = _tile(seq, 512)
    h = None
    for l in range(depth):
        if h is None:
            proj, (w_out, w_up, w_down) = _in_proj(
                xf, norm_mix_pre[l], w_in, l, tm=tm_small, cast_along=(w_out, w_up, w_down))
        else:
            proj = _in_proj(h, None, w_in, l, tm=tm_small)
        o_ret = _retention(proj, ret_gn_w[l], batch, seq, rows=_tile(seq, 4096))
        o_sb = _stick_breaking(proj, batch, seq, rows=_tile(seq, 2048))
        o_swa = _sliding_window(proj, swa_sinks[l], batch, seq, rows=_tile(seq, 2048))
        xf, h = _out_proj(o_ret, o_sb, o_swa, w_out, l, norm_mix_post[l], norm_ffn_pre[l], xf, tm=tm_small)
        g = _ffn_up(h, w_up, l, conv_w[l], conv_b[l], seq, tm=tm_small)
        next_gain = norm_mix_pre[l + 1] if l + 1 < depth else None
        xf, h = _ffn_down(g, w_down, l, norm_ffn_post[l], next_gain, xf, tm=tm_small)
    return xf.reshape(batch, seq, d_model).astype(x.dtype)
```
